```python
import jax, jax.numpy as jnp
from jax import lax
import numpy as np

D_MODEL = 2048
BATCH = 4
SEQ = 2048
DEPTH = 4
DEC_BATCH = 128
DEC_SEQ = 8
PAST_LEN = 16384
PAGE_SIZE = 128

N_MIXERS = 3
N_LAYERS_A = (DEPTH + 2) // 3
N_LAYERS_B = (DEPTH + 1) // 3
N_LAYERS_C = DEPTH // 3
D_FF = ((8 * D_MODEL // 3 + 127) // 128) * 128
CHUNK = 128
D_A = D_MODEL
N_GROUPS_A = D_A // 128
HEAD_A = D_A // N_GROUPS_A
POOL_WINDOWS = (2, 4, 8, 16)
N_POOL_GROUPS = 4
POOL_GROUP = D_MODEL // N_POOL_GROUPS
POOL_HIST = 15
D_C = D_MODEL
CONV_W = 3
ALPHA = (2 * DEPTH) ** 0.25
BETA = (8 * DEPTH) ** -0.25
LN_EPS = 1e-5

kernel_name = 'hybrid_chunkmlp_pool_conv_decoder'


def layer_norm(x, g, b):
    xf = x.astype(jnp.float32)
    mu = xf.mean(-1, keepdims=True)
    var = jnp.square(xf - mu).mean(-1, keepdims=True)
    return ((xf - mu) * lax.rsqrt(var + LN_EPS) * g.astype(jnp.float32) + b.astype(jnp.float32)).astype(x.dtype)


def post_norm(x, delta, g, b):
    return layer_norm(ALPHA * x + delta, g, b)


def swiglu(x, w_gu, w_down):
    gate, up = jnp.split(x @ w_gu, 2, axis=-1)
    return (jax.nn.silu(gate) * up) @ w_down


def chunk_mlp_mixer(x, w_in, ln_g, ln_b, w_s, b_s, w_out):
    bsz, t, _ = x.shape
    u, v = jnp.split(jax.nn.gelu(x @ w_in, approximate=False), 2, axis=-1)
    v = layer_norm(v, ln_g, ln_b)
    pad = (-t) % CHUNK
    n = (t + pad) // CHUNK
    vp = jnp.pad(v, ((0, 0), (0, pad), (0, 0))).reshape(bsz, n, CHUNK, N_GROUPS_A, HEAD_A)
    causal = jnp.tril(jnp.ones((CHUNK, CHUNK), dtype=bool))
    wm = jnp.where(causal[None], w_s, 0)
    mixed = jnp.einsum('hts,bnshd->bnthd', wm, vp) + b_s.T[:, :, None]
    mixed = mixed.reshape(bsz, n * CHUNK, D_A)[:, :t]
    return (u * mixed) @ w_out, v


def pool_mixer(x, hist, start_pos, w_grp, scale):
    bsz, t, d = x.shape
    xe = jnp.concatenate([hist.astype(x.dtype), x], axis=1).astype(jnp.float32)
    cs = jnp.pad(jnp.cumsum(xe, axis=1), ((0, 0), (1, 0), (0, 0)))
    pos = start_pos + jnp.arange(t)
    end = cs[:, POOL_HIST + 1:POOL_HIST + 1 + t]
    means = []
    for g, w in enumerate(POOL_WINDOWS):
        sl = slice(g * POOL_GROUP, (g + 1) * POOL_GROUP)
        begin = cs[:, POOL_HIST + 1 - w:POOL_HIST + 1 - w + t, sl]
        cnt = jnp.minimum(pos + 1, w).astype(jnp.float32)
        means.append((end[..., sl] - begin) / cnt[None, :, None])
    pooled = (jnp.concatenate(means, axis=-1) - xe[:, POOL_HIST:]).astype(x.dtype)
    pooled = pooled.reshape(bsz, t, N_POOL_GROUPS, POOL_GROUP)
    y = jnp.einsum('btgc,gcd->btgd', pooled, w_grp).reshape(bsz, t, d)
    return y * scale, xe[:, -POOL_HIST:].astype(x.dtype)


def conv_mixer(x, hist, w_in, w_conv, w_out):
    t = x.shape[1]
    b_gate, c_gate, h = jnp.split(x @ w_in, 3, axis=-1)
    z = c_gate * h
    ze = jnp.concatenate([hist.astype(z.dtype), z], axis=1)
    conv = w_conv[CONV_W - 1] * ze[:, CONV_W - 1:CONV_W - 1 + t]
    for k in range(CONV_W - 1):
        conv = conv + w_conv[k] * ze[:, k:k + t]
    return (b_gate * conv) @ w_out, ze[:, -(CONV_W - 1):]


def setup_inputs(seed: int = 0) -> dict:
    key = jax.random.key(seed)
    ks = jax.random.split(key, 32)

    def nrm(k, shape, scale):
        return jax.random.normal(k, shape, jnp.float32) * scale

    return {
        'x_prompt': nrm(ks[0], (BATCH, SEQ, D_MODEL), 1.0),
        'x_sample': nrm(ks[1], (DEC_BATCH, DEC_SEQ, D_MODEL), 1.0),
        'state_pool': nrm(ks[2], (N_LAYERS_B, DEC_BATCH, POOL_HIST, D_MODEL), 1.0),
        'state_conv': nrm(ks[3], (N_LAYERS_C, DEC_BATCH, CONV_W - 1, D_C), 1.0),
        'ffn1_w_gu': nrm(ks[4], (DEPTH, D_MODEL, 2 * D_FF), D_MODEL ** -0.5),
        'ffn1_w_down': nrm(ks[5], (DEPTH, D_FF, D_MODEL), BETA * D_FF ** -0.5),
        'ffn2_w_gu': nrm(ks[6], (DEPTH, D_MODEL, 2 * D_FF), D_MODEL ** -0.5),
        'ffn2_w_down': nrm(ks[7], (DEPTH, D_FF, D_MODEL), BETA * D_FF ** -0.5),
        'ln1_g': 1.0 + nrm(ks[8], (DEPTH, D_MODEL), 0.05),
        'ln1_b': nrm(ks[9], (DEPTH, D_MODEL), 0.02),
        'ln2_g': 1.0 + nrm(ks[10], (DEPTH, D_MODEL), 0.05),
        'ln2_b': nrm(ks[11], (DEPTH, D_MODEL), 0.02),
        'ln3_g': 1.0 + nrm(ks[12], (DEPTH, D_MODEL), 0.05),
        'ln3_b': nrm(ks[13], (DEPTH, D_MODEL), 0.02),
        'a_w_in': nrm(ks[14], (N_LAYERS_A, D_MODEL, 2 * D_A), D_MODEL ** -0.5),
        'a_ln_g': 1.0 + nrm(ks[15], (N_LAYERS_A, D_A), 0.05),
        'a_ln_b': nrm(ks[16], (N_LAYERS_A, D_A), 0.02),
        'a_w_s': nrm(ks[17], (N_LAYERS_A, N_GROUPS_A, CHUNK, CHUNK), CHUNK ** -0.5),
        'a_b_s': 1.0 + nrm(ks[18], (N_LAYERS_A, N_GROUPS_A, CHUNK), 0.05),
        'a_w_out': nrm(ks[19], (N_LAYERS_A, D_A, D_MODEL), BETA * D_A ** -0.5),
        'b_w_grp': nrm(ks[20], (N_LAYERS_B, N_POOL_GROUPS, POOL_GROUP, POOL_GROUP), BETA * POOL_GROUP ** -0.5),
        'b_scale': 1.0 + nrm(ks[21], (N_LAYERS_B, D_MODEL), 0.1),
        'c_w_in': nrm(ks[22], (N_LAYERS_C, D_MODEL, 3 * D_C), D_MODEL ** -0.5),
        'c_w_conv': nrm(ks[23], (N_LAYERS_C, CONV_W, D_C), CONV_W ** -0.5),
        'c_w_out': nrm(ks[24], (N_LAYERS_C, D_C, D_MODEL), BETA * D_C ** -0.5),
    }


def reference(x_prompt, x_sample, state_pool, state_conv,
              ffn1_w_gu, ffn1_w_down, ffn2_w_gu, ffn2_w_down,
              ln1_g, ln1_b, ln2_g, ln2_b, ln3_g, ln3_b,
              a_w_in, a_ln_g, a_ln_b, a_w_s, a_b_s, a_w_out,
              b_w_grp, b_scale,
              c_w_in, c_w_conv, c_w_out):
    xp, xs = x_prompt, x_sample
    pool_p, pool_s, conv_p, conv_s, chunk_v_s = [], [], [], [], []
    for i in range(DEPTH):
        kind, j = i % N_MIXERS, i // N_MIXERS
        xp = post_norm(xp, 0.5 * swiglu(xp, ffn1_w_gu[i], ffn1_w_down[i]), ln1_g[i], ln1_b[i])
        xs = post_norm(xs, 0.5 * swiglu(xs, ffn1_w_gu[i], ffn1_w_down[i]), ln1_g[i], ln1_b[i])
        if kind == 0:
            mp, _ = chunk_mlp_mixer(xp, a_w_in[j], a_ln_g[j], a_ln_b[j], a_w_s[j], a_b_s[j], a_w_out[j])
            ms, vs = chunk_mlp_mixer(xs, a_w_in[j], a_ln_g[j], a_ln_b[j], a_w_s[j], a_b_s[j], a_w_out[j])
            chunk_v_s.append(vs)
        elif kind == 1:
            zero_hist = jnp.zeros((xp.shape[0], POOL_HIST, D_MODEL), xp.dtype)
            mp, hp = pool_mixer(xp, zero_hist, 0, b_w_grp[j], b_scale[j])
            ms, hs = pool_mixer(xs, state_pool[j], PAST_LEN, b_w_grp[j], b_scale[j])
            pool_p.append(hp)
            pool_s.append(hs)
        else:
            zero_hist = jnp.zeros((xp.shape[0], CONV_W - 1, D_C), xp.dtype)
            mp, hp = conv_mixer(xp, zero_hist, c_w_in[j], c_w_conv[j], c_w_out[j])
            ms, hs = conv_mixer(xs, state_conv[j], c_w_in[j], c_w_conv[j], c_w_out[j])
            conv_p.append(hp)
            conv_s.append(hs)
        xp = post_norm(xp, mp, ln2_g[i], ln2_b[i])
        xs = post_norm(xs, ms, ln2_g[i], ln2_b[i])
        xp = post_norm(xp, 0.5 * swiglu(xp, ffn2_w_gu[i], ffn2_w_down[i]), ln3_g[i], ln3_b[i])
        xs = post_norm(xs, 0.5 * swiglu(xs, ffn2_w_gu[i], ffn2_w_down[i]), ln3_g[i], ln3_b[i])
    return (xp, xs, jnp.stack(pool_p), jnp.stack(pool_s), jnp.stack(conv_p), jnp.stack(conv_s), jnp.stack(chunk_v_s))
```

```python
import functools
import math

import jax
import jax.numpy as jnp
from jax import lax
from jax.experimental import pallas as pl
from jax.experimental.pallas import tpu as pltpu

D_MODEL = 2048
DEPTH = 4
PAST_LEN = 16384
D_FF = 5504
CHUNK = 128
N_GROUPS_A = 16
HEAD_A = 128
POOL_WINDOWS = (2, 4, 8, 16)
POOL_GROUP = 512
POOL_HIST = 15
CONV_W = 3
ALPHA = (2 * DEPTH) ** 0.25
LN_EPS = 1e-5

SUBLANES = 8
LANES = 128
VMEM_LIMIT_BYTES = 56 * 1024 * 1024

FFN_TM = 1024
FFN_TF = 512
FFN_RC = 256
D_FF_PAD = ((D_FF + FFN_TF - 1) // FFN_TF) * FFN_TF
MIX_TM = 256

_BF16 = jnp.bfloat16
_F32 = jnp.float32


def _ln(y, g, b):
    mu = jnp.mean(y, axis=-1, keepdims=True)
    d = y - mu
    var = jnp.mean(d * d, axis=-1, keepdims=True)
    return d * lax.rsqrt(var + LN_EPS) * g + b


def _dot(a, b):
    return jnp.dot(a, b, preferred_element_type=_F32)


def _resident(shape):
    zeros = (0,) * len(shape)
    return pl.BlockSpec(shape, lambda *_: zeros, pipeline_mode=pl.Buffered(1))


def _ffn_kernel(x_ref, wg_ref, wu_ref, wd_ref, g_ref, b_ref, o_ref, xb_ref, *, nf, tm):
    j = pl.program_id(1)

    @pl.when(j == 0)
    def _():
        xb_ref[...] = x_ref[...].astype(_BF16)
        o_ref[...] = jnp.zeros_like(o_ref)

    for r in range(0, tm, FFN_RC):
        rows = pl.ds(r, FFN_RC)
        xb = xb_ref[rows, :]
        h = _dot(xb, wg_ref[...])
        u = _dot(xb, wu_ref[...])
        a = (jax.nn.silu(h) * u).astype(_BF16)
        o_ref[rows, :] += _dot(a, wd_ref[...])

    @pl.when(j == nf - 1)
    def _():
        for r in range(0, tm, FFN_RC):
            rows = pl.ds(r, FFN_RC)
            y = ALPHA * x_ref[rows, :] + 0.5 * o_ref[rows, :]
            o_ref[rows, :] = _ln(y, g_ref[...], b_ref[...])


def _ffn(x, wg, wu, wd, g, b):
    t = x.shape[0]
    tm = min(FFN_TM, t)
    nf = D_FF_PAD // FFN_TF
    return pl.pallas_call(
        functools.partial(_ffn_kernel, nf=nf, tm=tm),
        grid=(t // tm, nf),
        in_specs=[
            pl.BlockSpec((tm, D_MODEL), lambda i, j: (i, 0), pipeline_mode=pl.Buffered(1)),
            pl.BlockSpec((D_MODEL, FFN_TF), lambda i, j: (0, j)),
            pl.BlockSpec((D_MODEL, FFN_TF), lambda i, j: (0, j)),
            pl.BlockSpec((FFN_TF, D_MODEL), lambda i, j: (j, 0)),
            pl.BlockSpec((1, D_MODEL), lambda i, j: (0, 0)),
            pl.BlockSpec((1, D_MODEL), lambda i, j: (0, 0)),
        ],
        out_specs=pl.BlockSpec((tm, D_MODEL), lambda i, j: (i, 0)),
        out_shape=jax.ShapeDtypeStruct((t, D_MODEL), _F32),
        scratch_shapes=[pltpu.VMEM((tm, D_MODEL), _BF16)],
        compiler_params=pltpu.CompilerParams(
            dimension_semantics=("arbitrary", "arbitrary"),
            vmem_limit_bytes=VMEM_LIMIT_BYTES),
        name="ffn",
    )(x, wg, wu, wd, g, b)


def _mixer_a_kernel(x_ref, win_ref, lng_ref, lnb_ref, wm_ref, bias_ref, wout_ref,
                    g_ref, b_ref, *out_and_scratch, sample, tm):
    if sample:
        o_ref, v_ref, mix_ref = out_and_scratch
    else:
        o_ref, mix_ref = out_and_scratch
    x = x_ref[...]
    xb = x.astype(_BF16)
    sqrt_half = math.sqrt(0.5)
    pre_u = _dot(xb, win_ref[:, :D_MODEL])
    u = 0.5 * pre_u * (1.0 + lax.erf(pre_u * sqrt_half))
    pre_v = _dot(xb, win_ref[:, D_MODEL:])
    v = 0.5 * pre_v * (1.0 + lax.erf(pre_v * sqrt_half))
    v = _ln(v, lng_ref[...], lnb_ref[...])
    if sample:
        v_ref[...] = v
    vb = v.astype(_BF16)

    row = lax.broadcasted_iota(jnp.int32, (CHUNK, CHUNK), 0)
    col = lax.broadcasted_iota(jnp.int32, (CHUNK, CHUNK), 1)
    mask = col <= row
    if sample:
        mask = jnp.logical_and(mask, (row // SUBLANES) == (col // SUBLANES))
    for hd in range(N_GROUPS_A):
        wm = jnp.where(mask, wm_ref[hd], 0.0).astype(_BF16)
        cols = pl.ds(hd * HEAD_A, HEAD_A)
        for c in range(tm // CHUNK):
            vv = vb[c * CHUNK:(c + 1) * CHUNK, hd * HEAD_A:(hd + 1) * HEAD_A]
            mix_ref[pl.ds(c * CHUNK, CHUNK), cols] = _dot(wm, vv)
    for c in range(tm // CHUNK):
        rows = pl.ds(c * CHUNK, CHUNK)
        mix_ref[rows, :] = u[c * CHUNK:(c + 1) * CHUNK, :] * (mix_ref[rows, :] + bias_ref[...])
    m = _dot(mix_ref[...].astype(_BF16), wout_ref[...])
    o_ref[...] = _ln(ALPHA * x + m, g_ref[...], b_ref[...])


def _mixer_a(x, win, lng, lnb, wm, bias, wout, g, b, *, sample):
    t = x.shape[0]
    tm = MIX_TM
    row_spec = pl.BlockSpec((tm, D_MODEL), lambda i: (i, 0))
    out_shape = jax.ShapeDtypeStruct((t, D_MODEL), _F32)
    return pl.pallas_call(
        functools.partial(_mixer_a_kernel, sample=sample, tm=tm),
        grid=(t // tm,),
        in_specs=[
            row_spec,
            _resident((D_MODEL, 2 * D_MODEL)),
            _resident((1, D_MODEL)),
            _resident((1, D_MODEL)),
            _resident((N_GROUPS_A, CHUNK, CHUNK)),
            _resident((CHUNK, D_MODEL)),
            _resident((D_MODEL, D_MODEL)),
            _resident((1, D_MODEL)),
            _resident((1, D_MODEL)),
        ],
        out_specs=(row_spec, row_spec) if sample else row_spec,
        out_shape=(out_shape, out_shape) if sample else out_shape,
        scratch_shapes=[pltpu.VMEM((tm, D_MODEL), _F32)],
        compiler_params=pltpu.CompilerParams(
            dimension_semantics=("arbitrary",), vmem_limit_bytes=VMEM_LIMIT_BYTES),
        name="mixer_a_sample" if sample else "mixer_a_prompt",
    )(x, win, lng, lnb, wm, bias, wout, g, b)


def _pool_project(x, win_sums, cnts, w_ref, scale_ref, g_ref, b_ref, o_ref):
    parts = []
    for gi in range(len(POOL_WINDOWS)):
        xg = x[:, gi * POOL_GROUP:(gi + 1) * POOL_GROUP]
        pooled = (win_sums[gi] / cnts[gi] - xg).astype(_BF16)
        parts.append(_dot(pooled, w_ref[gi]))
    m = jnp.concatenate(parts, axis=1) * scale_ref[...]
    o_ref[...] = _ln(ALPHA * x + m, g_ref[...], b_ref[...])


def _mixer_b_prompt_kernel(x_ref, halo_ref, w_ref, scale_ref, g_ref, b_ref, o_ref, xe_ref,
                           *, tm, tiles_per_seq):
    halo_rows = 2 * SUBLANES
    i = pl.program_id(0)
    tile_in_seq = i % tiles_per_seq
    x = x_ref[...]
    xe_ref[pl.ds(0, halo_rows), :] = jnp.where(tile_in_seq == 0, 0.0, halo_ref[...])
    xe_ref[pl.ds(halo_rows, tm), :] = x
    pos = tile_in_seq * tm + lax.broadcasted_iota(jnp.int32, (tm, 1), 0)
    sums, cnts = [], []
    for gi, w in enumerate(POOL_WINDOWS):
        cols = pl.ds(gi * POOL_GROUP, POOL_GROUP)
        s = x[:, gi * POOL_GROUP:(gi + 1) * POOL_GROUP]
        for k in range(1, w):
            s = s + xe_ref[pl.ds(halo_rows - k, tm), cols]
        sums.append(s)
        cnts.append(jnp.minimum(pos + 1, w).astype(_F32))
    _pool_project(x, sums, cnts, w_ref, scale_ref, g_ref, b_ref, o_ref)


def _mixer_b_sample_kernel(x_ref, h1_ref, h2_ref, w_ref, scale_ref, g_ref, b_ref, o_ref, *, tm):
    x = x_ref[...]
    t = lax.broadcasted_iota(jnp.int32, (tm, 1), 0) % SUBLANES
    sums, cnts = [], []
    for gi, w in enumerate(POOL_WINDOWS):
        lo, hi = gi * POOL_GROUP, (gi + 1) * POOL_GROUP
        cur, h1, h2 = x[:, lo:hi], h1_ref[:, lo:hi], h2_ref[:, lo:hi]
        s = cur
        for k in range(1, w):
            kk = k % SUBLANES
            new, old = (cur, h1) if k < SUBLANES else (h1, h2)
            if kk == 0:
                s = s + new
            else:
                s = s + jnp.where(t >= kk, pltpu.roll(new, kk, 0),
                                  pltpu.roll(old, tm + kk - SUBLANES, 0))
        sums.append(s)
        cnts.append(jnp.minimum(PAST_LEN + t + 1, w).astype(_F32))
    _pool_project(x, sums, cnts, w_ref, scale_ref, g_ref, b_ref, o_ref)


def _mixer_b_prompt(x, w, scale, g, b, *, seq):
    t = x.shape[0]
    tm = MIX_TM
    halo_rows = 2 * SUBLANES
    ratio = tm // halo_rows
    return pl.pallas_call(
        functools.partial(_mixer_b_prompt_kernel, tm=tm, tiles_per_seq=seq // tm),
        grid=(t // tm,),
        in_specs=[
            pl.BlockSpec((tm, D_MODEL), lambda i: (i, 0)),
            pl.BlockSpec((halo_rows, D_MODEL), lambda i: (jnp.maximum(i * ratio - 1, 0), 0)),
            _resident((len(POOL_WINDOWS), POOL_GROUP, POOL_GROUP)),
            _resident((1, D_MODEL)),
            _resident((1, D_MODEL)),
            _resident((1, D_MODEL)),
        ],
        out_specs=pl.BlockSpec((tm, D_MODEL), lambda i: (i, 0)),
        out_shape=jax.ShapeDtypeStruct((t, D_MODEL), _F32),
        scratch_shapes=[pltpu.VMEM((tm + halo_rows, D_MODEL), _F32)],
        compiler_params=pltpu.CompilerParams(
            dimension_semantics=("arbitrary",), vmem_limit_bytes=VMEM_LIMIT_BYTES),
        name="mixer_b_prompt",
    )(x, x, w, scale, g, b)


def _mixer_b_sample(x, h1, h2, w, scale, g, b):
    t = x.shape[0]
    tm = MIX_TM
    row_spec = pl.BlockSpec((tm, D_MODEL), lambda i: (i, 0))
    return pl.pallas_call(
        functools.partial(_mixer_b_sample_kernel, tm=tm),
        grid=(t // tm,),
        in_specs=[
            row_spec, row_spec, row_spec,
            _resident((len(POOL_WINDOWS), POOL_GROUP, POOL_GROUP)),
            _resident((1, D_MODEL)),
            _resident((1, D_MODEL)),
            _resident((1, D_MODEL)),
        ],
        out_specs=row_spec,
        out_shape=jax.ShapeDtypeStruct((t, D_MODEL), _F32),
        compiler_params=pltpu.CompilerParams(
            dimension_semantics=("arbitrary",), vmem_limit_bytes=VMEM_LIMIT_BYTES),
        name="mixer_b_sample",
    )(x, h1, h2, w, scale, g, b)


def _conv_project(x, xb, z, z1, z2, win_ref, wconv_ref, wout_ref, g_ref, b_ref, o_ref):
    conv = wconv_ref[2:3, :] * z + wconv_ref[0:1, :] * z2 + wconv_ref[1:2, :] * z1
    bq = _dot(xb, win_ref[:, :D_MODEL])
    m = _dot((bq * conv).astype(_BF16), wout_ref[...])
    o_ref[...] = _ln(ALPHA * x + m, g_ref[...], b_ref[...])


def _gate_product(xb, win_ref):
    c = _dot(xb, win_ref[:, D_MODEL:2 * D_MODEL])
    h = _dot(xb, win_ref[:, 2 * D_MODEL:])
    return c * h


def _mixer_c_prompt_kernel(x_ref, win_ref, wconv_ref, wout_ref, g_ref, b_ref,
                           o_ref, ztail_ref, ze_ref, carry_ref, *, tm, tiles_per_seq):
    i = pl.program_id(0)
    x = x_ref[...]
    xb = x.astype(_BF16)
    z = _gate_product(xb, win_ref)
    first = (i % tiles_per_seq) == 0
    ze_ref[pl.ds(0, SUBLANES), :] = jnp.where(first, 0.0, carry_ref[...])
    ze_ref[pl.ds(SUBLANES, tm), :] = z
    tail = z[tm - SUBLANES:, :]
    carry_ref[...] = tail
    ztail_ref[...] = tail
    z1 = ze_ref[pl.ds(SUBLANES - 1, tm), :]
    z2 = ze_ref[pl.ds(SUBLANES - 2, tm), :]
    _conv_project(x, xb, z, z1, z2, win_ref, wconv_ref, wout_ref, g_ref, b_ref, o_ref)


def _mixer_c_sample_kernel(x_ref, h1_ref, win_ref, wconv_ref, wout_ref, g_ref, b_ref,
                           o_ref, z_ref, *, tm):
    x = x_ref[...]
    xb = x.astype(_BF16)
    z = _gate_product(xb, win_ref)
    z_ref[...] = z
    t = lax.broadcasted_iota(jnp.int32, (tm, 1), 0) % SUBLANES
    h1 = h1_ref[...]
    z1 = jnp.where(t >= 1, pltpu.roll(z, 1, 0), pltpu.roll(h1, tm + 1 - SUBLANES, 0))
    z2 = jnp.where(t >= 2, pltpu.roll(z, 2, 0), pltpu.roll(h1, tm + 2 - SUBLANES, 0))
    _conv_project(x, xb, z, z1, z2, win_ref, wconv_ref, wout_ref, g_ref, b_ref, o_ref)


def _mixer_c_specs():
    return [
        _resident((D_MODEL, 3 * D_MODEL)),
        _resident((CONV_W, D_MODEL)),
        _resident((D_MODEL, D_MODEL)),
        _resident((1, D_MODEL)),
        _resident((1, D_MODEL)),
    ]


def _mixer_c_prompt(x, win, wconv, wout, g, b, *, seq):
    t = x.shape[0]
    tm = MIX_TM
    row_spec = pl.BlockSpec((tm, D_MODEL), lambda i: (i, 0))
    return pl.pallas_call(
        functools.partial(_mixer_c_prompt_kernel, tm=tm, tiles_per_seq=seq // tm),
        grid=(t // tm,),
        in_specs=[row_spec] + _mixer_c_specs(),
        out_specs=(row_spec, pl.BlockSpec((SUBLANES, D_MODEL), lambda i: (i, 0))),
        out_shape=(jax.ShapeDtypeStruct((t, D_MODEL), _F32),
                   jax.ShapeDtypeStruct((t // tm * SUBLANES, D_MODEL), _F32)),
        scratch_shapes=[pltpu.VMEM((tm + SUBLANES, D_MODEL), _F32),
                        pltpu.VMEM((SUBLANES, D_MODEL), _F32)],
        compiler_params=pltpu.CompilerParams(
            dimension_semantics=("arbitrary",), vmem_limit_bytes=VMEM_LIMIT_BYTES),
        name="mixer_c_prompt",
    )(x, win, wconv, wout, g, b)


def _mixer_c_sample(x, h1, win, wconv, wout, g, b):
    t = x.shape[0]
    tm = MIX_TM
    row_spec = pl.BlockSpec((tm, D_MODEL), lambda i: (i, 0))
    out_shape = jax.ShapeDtypeStruct((t, D_MODEL), _F32)
    return pl.pallas_call(
        functools.partial(_mixer_c_sample_kernel, tm=tm),
        grid=(t // tm,),
        in_specs=[row_spec, row_spec] + _mixer_c_specs(),
        out_specs=(row_spec, row_spec),
        out_shape=(out_shape, out_shape),
        compiler_params=pltpu.CompilerParams(
            dimension_semantics=("arbitrary",), vmem_limit_bytes=VMEM_LIMIT_BYTES),
        name="mixer_c_sample",
    )(x, h1, win, wconv, wout, g, b)


def _row(v):
    return v.reshape(1, -1)


def _ffn_weights(w_gu, w_down):
    pad = D_FF_PAD - D_FF
    wg = jnp.pad(w_gu[:, :D_FF].astype(_BF16), ((0, 0), (0, pad)))
    wu = jnp.pad(w_gu[:, D_FF:].astype(_BF16), ((0, 0), (0, pad)))
    wd = jnp.pad(w_down.astype(_BF16), ((0, pad), (0, 0)))
    return wg, wu, wd


def _front_pad_hist(hist, rows):
    bsz, h, d = hist.shape
    return jnp.pad(hist, ((0, 0), (rows - h, 0), (0, 0))).reshape(bsz * rows, d)


def kernel(x_prompt, x_sample, state_pool, state_conv, ffn1_w_gu, ffn1_w_down, ffn2_w_gu, ffn2_w_down, ln1_g, ln1_b, ln2_g, ln2_b, ln3_g, ln3_b, a_w_in, a_ln_g, a_ln_b, a_w_s, a_b_s, a_w_out, b_w_grp, b_scale, c_w_in, c_w_conv, c_w_out):
    bp, seq, d = x_prompt.shape
    bs, dec_seq, _ = x_sample.shape
    assert dec_seq == SUBLANES and seq % MIX_TM == 0 and (bs * dec_seq) % MIX_TM == 0
    xp = x_prompt.reshape(bp * seq, d)
    xs = x_sample.reshape(bs * dec_seq, d)
    pool_p, pool_s, conv_p, conv_s, chunk_v_s = [], [], [], [], []
    for i in range(DEPTH):
        kind, j = i % 3, i // 3
        w1 = _ffn_weights(ffn1_w_gu[i], ffn1_w_down[i])
        xp = _ffn(xp, *w1, _row(ln1_g[i]), _row(ln1_b[i]))
        xs = _ffn(xs, *w1, _row(ln1_g[i]), _row(ln1_b[i]))
        g2, b2 = _row(ln2_g[i]), _row(ln2_b[i])
        if kind == 0:
            win = a_w_in[j].astype(_BF16)
            wout = a_w_out[j].astype(_BF16)
            lng, lnb = _row(a_ln_g[j]), _row(a_ln_b[j])
            bias = jnp.repeat(a_b_s[j].T, HEAD_A, axis=1)
            reps = CHUNK // dec_seq
            wm_s = jnp.tile(a_w_s[j][:, :dec_seq, :dec_seq], (1, reps, reps))
            bias_s = jnp.tile(bias[:dec_seq], (reps, 1))
            xp = _mixer_a(xp, win, lng, lnb, a_w_s[j], bias, wout, g2, b2, sample=False)
            xs, vs = _mixer_a(xs, win, lng, lnb, wm_s, bias_s, wout, g2, b2, sample=True)
            chunk_v_s.append(vs.reshape(bs, dec_seq, d))
        elif kind == 1:
            w = b_w_grp[j].astype(_BF16)
            scale = _row(b_scale[j])
            hist = state_pool[j]
            pool_p.append(xp.reshape(bp, seq, d)[:, seq - POOL_HIST:, :])
            pool_s.append(jnp.concatenate(
                [hist[:, dec_seq:, :], xs.reshape(bs, dec_seq, d)], axis=1)[:, -POOL_HIST:, :])
            hist16 = _front_pad_hist(hist, 2 * SUBLANES).reshape(bs, 2, SUBLANES, d)
            h2 = hist16[:, 0].reshape(bs * SUBLANES, d)
            h1 = hist16[:, 1].reshape(bs * SUBLANES, d)
            xp = _mixer_b_prompt(xp, w, scale, g2, b2, seq=seq)
            xs = _mixer_b_sample(xs, h1, h2, w, scale, g2, b2)
        else:
            win = c_w_in[j].astype(_BF16)
            wout = c_w_out[j].astype(_BF16)
            h1 = _front_pad_hist(state_conv[j], SUBLANES)
            xp, ztail = _mixer_c_prompt(xp, win, c_w_conv[j], wout, g2, b2, seq=seq)
            xs, zs = _mixer_c_sample(xs, h1, win, c_w_conv[j], wout, g2, b2)
            tiles_per_seq = seq // MIX_TM
            ztail = ztail.reshape(bp, tiles_per_seq, SUBLANES, d)
            conv_p.append(ztail[:, -1, SUBLANES - (CONV_W - 1):, :])
            conv_s.append(zs.reshape(bs, dec_seq, d)[:, dec_seq - (CONV_W - 1):, :])
        w2 = _ffn_weights(ffn2_w_gu[i], ffn2_w_down[i])
        xp = _ffn(xp, *w2, _row(ln3_g[i]), _row(ln3_b[i]))
        xs = _ffn(xs, *w2, _row(ln3_g[i]), _row(ln3_b[i]))
    return (xp.reshape(bp, seq, d), xs.reshape(bs, dec_seq, d),
            jnp.stack(pool_p), jnp.stack(pool_s), jnp.stack(conv_p), jnp.stack(conv_s),
            jnp.stack(chunk_v_s))
```

```python
import functools
import math

import jax
import jax.numpy as jnp
from jax import lax
from jax.experimental import pallas as pl
from jax.experimental.pallas import tpu as pltpu

D_MODEL = 2048
DEPTH = 4
PAST_LEN = 16384
D_FF = 5504
CHUNK = 128
N_GROUPS_A = 16
HEAD_A = 128
POOL_WINDOWS = (2, 4, 8, 16)
POOL_GROUP = 512
POOL_HIST = 15
CONV_W = 3
ALPHA = (2 * DEPTH) ** 0.25
LN_EPS = 1e-5

SUBLANES = 8
LANES = 128
VMEM_LIMIT_BYTES = 56 * 1024 * 1024

FFN_TM = 1024
FFN_TF = 512
FFN_RC = 256
MIX_TM = 256

_BF16 = jnp.bfloat16
_F32 = jnp.float32


def _ln(y, g, b):
    mu = jnp.mean(y, axis=-1, keepdims=True)
    d = y - mu
    var = jnp.mean(d * d, axis=-1, keepdims=True)
    return d * lax.rsqrt(var + LN_EPS) * g + b


def _dot(a, b):
    return jnp.dot(a, b, preferred_element_type=_F32)


def _resident(shape):
    zeros = (0,) * len(shape)
    return pl.BlockSpec(shape, lambda *_: zeros, pipeline_mode=pl.Buffered(1))


def _ffn_kernel(x_ref, wg_ref, wu_ref, wd_ref, g_ref, b_ref, o_ref, *, nf, tm):
    j = pl.program_id(1)

    @pl.when(j == 0)
    def _():
        o_ref[...] = jnp.zeros_like(o_ref)

    def accumulate(lo):
        wg = wg_ref[:, lo:].astype(_BF16)
        wu = wu_ref[:, lo:].astype(_BF16)
        wd = wd_ref[lo:, :].astype(_BF16)
        for r in range(0, tm, FFN_RC):
            rows = pl.ds(r, FFN_RC)
            xb = x_ref[rows, :].astype(_BF16)
            h = _dot(xb, wg)
            u = _dot(xb, wu)
            a = (jax.nn.silu(h) * u).astype(_BF16)
            o_ref[rows, :] += _dot(a, wd)

    @pl.when(j < nf - 1)
    def _():
        accumulate(0)

    @pl.when(j == nf - 1)
    def _():
        accumulate(nf * FFN_TF - D_FF)
        for r in range(0, tm, FFN_RC):
            rows = pl.ds(r, FFN_RC)
            y = ALPHA * x_ref[rows, :] + 0.5 * o_ref[rows, :]
            o_ref[rows, :] = _ln(y, g_ref[...], b_ref[...])


def _ff_window_start(j, base=0):
    lane_tiles = jnp.minimum(j * (FFN_TF // LANES), (D_FF - FFN_TF) // LANES)
    return (lane_tiles + base // LANES) * LANES


def _ffn(x, w_gu, w_down, g, b):
    t = x.shape[0]
    tm = min(FFN_TM, t)
    nf = pl.cdiv(D_FF, FFN_TF)
    col_window = (pl.Element(D_MODEL), pl.Element(FFN_TF))
    row_window = (pl.Element(FFN_TF), pl.Element(D_MODEL))
    return pl.pallas_call(
        functools.partial(_ffn_kernel, nf=nf, tm=tm),
        grid=(t // tm, nf),
        in_specs=[
            pl.BlockSpec((tm, D_MODEL), lambda i, j: (i, 0), pipeline_mode=pl.Buffered(1)),
            pl.BlockSpec(col_window, lambda i, j: (0, _ff_window_start(j))),
            pl.BlockSpec(col_window, lambda i, j: (0, _ff_window_start(j, base=D_FF))),
            pl.BlockSpec(row_window, lambda i, j: (_ff_window_start(j), 0)),
            pl.BlockSpec((1, D_MODEL), lambda i, j: (0, 0)),
            pl.BlockSpec((1, D_MODEL), lambda i, j: (0, 0)),
        ],
        out_specs=pl.BlockSpec((tm, D_MODEL), lambda i, j: (i, 0)),
        out_shape=jax.ShapeDtypeStruct((t, D_MODEL), _F32),
        compiler_params=pltpu.CompilerParams(
            dimension_semantics=("arbitrary", "arbitrary"),
            vmem_limit_bytes=VMEM_LIMIT_BYTES),
        name="ffn",
    )(x, w_gu, w_gu, w_down, g, b)


def _mixer_a_kernel(x_ref, win_ref, lng_ref, lnb_ref, wm_ref, bias_ref, wout_ref,
                    g_ref, b_ref, *out_and_scratch, sample, tm):
    if sample:
        o_ref, v_ref, mix_ref = out_and_scratch
    else:
        o_ref, mix_ref = out_and_scratch
    x = x_ref[...]
    xb = x.astype(_BF16)
    sqrt_half = math.sqrt(0.5)
    pre_u = _dot(xb, win_ref[:, :D_MODEL])
    u = 0.5 * pre_u * (1.0 + lax.erf(pre_u * sqrt_half))
    pre_v = _dot(xb, win_ref[:, D_MODEL:])
    v = 0.5 * pre_v * (1.0 + lax.erf(pre_v * sqrt_half))
    v = _ln(v, lng_ref[...], lnb_ref[...])
    if sample:
        v_ref[...] = v
    vb = v.astype(_BF16)

    row = lax.broadcasted_iota(jnp.int32, (CHUNK, CHUNK), 0)
    col = lax.broadcasted_iota(jnp.int32, (CHUNK, CHUNK), 1)
    mask = col <= row
    if sample:
        mask = jnp.logical_and(mask, (row // SUBLANES) == (col // SUBLANES))
    for hd in range(N_GROUPS_A):
        wm = jnp.where(mask, wm_ref[hd], 0.0).astype(_BF16)
        cols = pl.ds(hd * HEAD_A, HEAD_A)
        for c in range(tm // CHUNK):
            vv = vb[c * CHUNK:(c + 1) * CHUNK, hd * HEAD_A:(hd + 1) * HEAD_A]
            mix_ref[pl.ds(c * CHUNK, CHUNK), cols] = _dot(wm, vv)
    for c in range(tm // CHUNK):
        rows = pl.ds(c * CHUNK, CHUNK)
        mix_ref[rows, :] = u[c * CHUNK:(c + 1) * CHUNK, :] * (mix_ref[rows, :] + bias_ref[...])
    m = _dot(mix_ref[...].astype(_BF16), wout_ref[...])
    o_ref[...] = _ln(ALPHA * x + m, g_ref[...], b_ref[...])


def _mixer_a(x, win, lng, lnb, wm, bias, wout, g, b, *, sample):
    t = x.shape[0]
    tm = MIX_TM
    row_spec = pl.BlockSpec((tm, D_MODEL), lambda i: (i, 0))
    out_shape = jax.ShapeDtypeStruct((t, D_MODEL), _F32)
    return pl.pallas_call(
        functools.partial(_mixer_a_kernel, sample=sample, tm=tm),
        grid=(t // tm,),
        in_specs=[
            row_spec,
            _resident((D_MODEL, 2 * D_MODEL)),
            _resident((1, D_MODEL)),
            _resident((1, D_MODEL)),
            _resident((N_GROUPS_A, CHUNK, CHUNK)),
            _resident((CHUNK, D_MODEL)),
            _resident((D_MODEL, D_MODEL)),
            _resident((1, D_MODEL)),
            _resident((1, D_MODEL)),
        ],
        out_specs=(row_spec, row_spec) if sample else row_spec,
        out_shape=(out_shape, out_shape) if sample else out_shape,
        scratch_shapes=[pltpu.VMEM((tm, D_MODEL), _F32)],
        compiler_params=pltpu.CompilerParams(
            dimension_semantics=("arbitrary",), vmem_limit_bytes=VMEM_LIMIT_BYTES),
        name="mixer_a_sample" if sample else "mixer_a_prompt",
    )(x, win, lng, lnb, wm, bias, wout, g, b)


def _pool_project(x, win_sums, cnts, w_ref, scale_ref, g_ref, b_ref, o_ref):
    parts = []
    for gi in range(len(POOL_WINDOWS)):
        xg = x[:, gi * POOL_GROUP:(gi + 1) * POOL_GROUP]
        pooled = (win_sums[gi] / cnts[gi] - xg).astype(_BF16)
        parts.append(_dot(pooled, w_ref[gi]))
    m = jnp.concatenate(parts, axis=1) * scale_ref[...]
    o_ref[...] = _ln(ALPHA * x + m, g_ref[...], b_ref[...])


def _mixer_b_prompt_kernel(x_ref, halo_ref, w_ref, scale_ref, g_ref, b_ref, o_ref, xe_ref,
                           *, tm, tiles_per_seq):
    halo_rows = 2 * SUBLANES
    i = pl.program_id(0)
    tile_in_seq = i % tiles_per_seq
    x = x_ref[...]
    xe_ref[pl.ds(0, halo_rows), :] = jnp.where(tile_in_seq == 0, 0.0, halo_ref[...])
    xe_ref[pl.ds(halo_rows, tm), :] = x
    pos = tile_in_seq * tm + lax.broadcasted_iota(jnp.int32, (tm, 1), 0)
    sums, cnts = [], []
    for gi, w in enumerate(POOL_WINDOWS):
        cols = pl.ds(gi * POOL_GROUP, POOL_GROUP)
        s = x[:, gi * POOL_GROUP:(gi + 1) * POOL_GROUP]
        for k in range(1, w):
            s = s + xe_ref[pl.ds(halo_rows - k, tm), cols]
        sums.append(s)
        cnts.append(jnp.minimum(pos + 1, w).astype(_F32))
    _pool_project(x, sums, cnts, w_ref, scale_ref, g_ref, b_ref, o_ref)


def _mixer_b_sample_kernel(x_ref, h1_ref, h2_ref, w_ref, scale_ref, g_ref, b_ref, o_ref, *, tm):
    x = x_ref[...]
    t = lax.broadcasted_iota(jnp.int32, (tm, 1), 0) % SUBLANES
    sums, cnts = [], []
    for gi, w in enumerate(POOL_WINDOWS):
        lo, hi = gi * POOL_GROUP, (gi + 1) * POOL_GROUP
        cur, h1, h2 = x[:, lo:hi], h1_ref[:, lo:hi], h2_ref[:, lo:hi]
        s = cur
        for k in range(1, w):
            kk = k % SUBLANES
            new, old = (cur, h1) if k < SUBLANES else (h1, h2)
            if kk == 0:
                s = s + new
            else:
                s = s + jnp.where(t >= kk, pltpu.roll(new, kk, 0),
                                  pltpu.roll(old, tm + kk - SUBLANES, 0))
        sums.append(s)
        cnts.append(jnp.minimum(PAST_LEN + t + 1, w).astype(_F32))
    _pool_project(x, sums, cnts, w_ref, scale_ref, g_ref, b_ref, o_ref)


def _mixer_b_prompt(x, w, scale, g, b, *, seq):
    t = x.shape[0]
    tm = MIX_TM
    halo_rows = 2 * SUBLANES
    ratio = tm // halo_rows
    return pl.pallas_call(
        functools.partial(_mixer_b_prompt_kernel, tm=tm, tiles_per_seq=seq // tm),
        grid=(t // tm,),
        in_specs=[
            pl.BlockSpec((tm, D_MODEL), lambda i: (i, 0)),
            pl.BlockSpec((halo_rows, D_MODEL), lambda i: (jnp.maximum(i * ratio - 1, 0), 0)),
            _resident((len(POOL_WINDOWS), POOL_GROUP, POOL_GROUP)),
            _resident((1, D_MODEL)),
            _resident((1, D_MODEL)),
            _resident((1, D_MODEL)),
        ],
        out_specs=pl.BlockSpec((tm, D_MODEL), lambda i: (i, 0)),
        out_shape=jax.ShapeDtypeStruct((t, D_MODEL), _F32),
        scratch_shapes=[pltpu.VMEM((tm + halo_rows, D_MODEL), _F32)],
        compiler_params=pltpu.CompilerParams(
            dimension_semantics=("arbitrary",), vmem_limit_bytes=VMEM_LIMIT_BYTES),
        name="mixer_b_prompt",
    )(x, x, w, scale, g, b)


def _mixer_b_sample(x, h1, h2, w, scale, g, b):
    t = x.shape[0]
    tm = MIX_TM
    row_spec = pl.BlockSpec((tm, D_MODEL), lambda i: (i, 0))
    return pl.pallas_call(
        functools.partial(_mixer_b_sample_kernel, tm=tm),
        grid=(t // tm,),
        in_specs=[
            row_spec, row_spec, row_spec,
            _resident((len(POOL_WINDOWS), POOL_GROUP, POOL_GROUP)),
            _resident((1, D_MODEL)),
            _resident((1, D_MODEL)),
            _resident((1, D_MODEL)),
        ],
        out_specs=row_spec,
        out_shape=jax.ShapeDtypeStruct((t, D_MODEL), _F32),
        compiler_params=pltpu.CompilerParams(
            dimension_semantics=("arbitrary",), vmem_limit_bytes=VMEM_LIMIT_BYTES),
        name="mixer_b_sample",
    )(x, h1, h2, w, scale, g, b)


def _conv_project(x, xb, z, z1, z2, win_ref, wconv_ref, wout_ref, g_ref, b_ref, o_ref):
    conv = wconv_ref[2:3, :] * z + wconv_ref[0:1, :] * z2 + wconv_ref[1:2, :] * z1
    bq = _dot(xb, win_ref[:, :D_MODEL])
    m = _dot((bq * conv).astype(_BF16), wout_ref[...])
    o_ref[...] = _ln(ALPHA * x + m, g_ref[...], b_ref[...])


def _gate_product(xb, win_ref):
    c = _dot(xb, win_ref[:, D_MODEL:2 * D_MODEL])
    h = _dot(xb, win_ref[:, 2 * D_MODEL:])
    return c * h


def _mixer_c_prompt_kernel(x_ref, win_ref, wconv_ref, wout_ref, g_ref, b_ref,
                           o_ref, ztail_ref, ze_ref, carry_ref, *, tm, tiles_per_seq):
    i = pl.program_id(0)
    x = x_ref[...]
    xb = x.astype(_BF16)
    z = _gate_product(xb, win_ref)
    first = (i % tiles_per_seq) == 0
    ze_ref[pl.ds(0, SUBLANES), :] = jnp.where(first, 0.0, carry_ref[...])
    ze_ref[pl.ds(SUBLANES, tm), :] = z
    tail = z[tm - SUBLANES:, :]
    carry_ref[...] = tail
    ztail_ref[...] = tail
    z1 = ze_ref[pl.ds(SUBLANES - 1, tm), :]
    z2 = ze_ref[pl.ds(SUBLANES - 2, tm), :]
    _conv_project(x, xb, z, z1, z2, win_ref, wconv_ref, wout_ref, g_ref, b_ref, o_ref)


def _mixer_c_sample_kernel(x_ref, h1_ref, win_ref, wconv_ref, wout_ref, g_ref, b_ref,
                           o_ref, z_ref, *, tm):
    x = x_ref[...]
    xb = x.astype(_BF16)
    z = _gate_product(xb, win_ref)
    z_ref[...] = z
    t = lax.broadcasted_iota(jnp.int32, (tm, 1), 0) % SUBLANES
    h1 = h1_ref[...]
    z1 = jnp.where(t >= 1, pltpu.roll(z, 1, 0), pltpu.roll(h1, tm + 1 - SUBLANES, 0))
    z2 = jnp.where(t >= 2, pltpu.roll(z, 2, 0), pltpu.roll(h1, tm + 2 - SUBLANES, 0))
    _conv_project(x, xb, z, z1, z2, win_ref, wconv_ref, wout_ref, g_ref, b_ref, o_ref)


def _mixer_c_specs():
    return [
        _resident((D_MODEL, 3 * D_MODEL)),
        _resident((CONV_W, D_MODEL)),
        _resident((D_MODEL, D_MODEL)),
        _resident((1, D_MODEL)),
        _resident((1, D_MODEL)),
    ]


def _mixer_c_prompt(x, win, wconv, wout, g, b, *, seq):
    t = x.shape[0]
    tm = MIX_TM
    row_spec = pl.BlockSpec((tm, D_MODEL), lambda i: (i, 0))
    return pl.pallas_call(
        functools.partial(_mixer_c_prompt_kernel, tm=tm, tiles_per_seq=seq // tm),
        grid=(t // tm,),
        in_specs=[row_spec] + _mixer_c_specs(),
        out_specs=(row_spec, pl.BlockSpec((SUBLANES, D_MODEL), lambda i: (i, 0))),
        out_shape=(jax.ShapeDtypeStruct((t, D_MODEL), _F32),
                   jax.ShapeDtypeStruct((t // tm * SUBLANES, D_MODEL), _F32)),
        scratch_shapes=[pltpu.VMEM((tm + SUBLANES, D_MODEL), _F32),
                        pltpu.VMEM((SUBLANES, D_MODEL), _F32)],
        compiler_params=pltpu.CompilerParams(
            dimension_semantics=("arbitrary",), vmem_limit_bytes=VMEM_LIMIT_BYTES),
        name="mixer_c_prompt",
    )(x, win, wconv, wout, g, b)


def _mixer_c_sample(x, h1, win, wconv, wout, g, b):
    t = x.shape[0]
    tm = MIX_TM
    row_spec = pl.BlockSpec((tm, D_MODEL), lambda i: (i, 0))
    out_shape = jax.ShapeDtypeStruct((t, D_MODEL), _F32)
    return pl.pallas_call(
        functools.partial(_mixer_c_sample_kernel, tm=tm),
        grid=(t // tm,),
        in_specs=[row_spec, row_spec] + _mixer_c_specs(),
        out_specs=(row_spec, row_spec),
        out_shape=(out_shape, out_shape),
        compiler_params=pltpu.CompilerParams(
            dimension_semantics=("arbitrary",), vmem_limit_bytes=VMEM_LIMIT_BYTES),
        name="mixer_c_sample",
    )(x, h1, win, wconv, wout, g, b)


def _row(v):
    return v.reshape(1, -1)


def _front_pad_hist(hist, rows):
    bsz, h, d = hist.shape
    return jnp.pad(hist, ((0, 0), (rows - h, 0), (0, 0))).reshape(bsz * rows, d)


def kernel(x_prompt, x_sample, state_pool, state_conv, ffn1_w_gu, ffn1_w_down, ffn2_w_gu, ffn2_w_down, ln1_g, ln1_b, ln2_g, ln2_b, ln3_g, ln3_b, a_w_in, a_ln_g, a_ln_b, a_w_s, a_b_s, a_w_out, b_w_grp, b_scale, c_w_in, c_w_conv, c_w_out):
    bp, seq, d = x_prompt.shape
    bs, dec_seq, _ = x_sample.shape
    assert dec_seq == SUBLANES and seq % MIX_TM == 0 and (bs * dec_seq) % MIX_TM == 0
    xp = x_prompt.reshape(bp * seq, d)
    xs = x_sample.reshape(bs * dec_seq, d)
    pool_p, pool_s, conv_p, conv_s, chunk_v_s = [], [], [], [], []
    for i in range(DEPTH):
        kind, j = i % 3, i // 3
        xp = _ffn(xp, ffn1_w_gu[i], ffn1_w_down[i], _row(ln1_g[i]), _row(ln1_b[i]))
        xs = _ffn(xs, ffn1_w_gu[i], ffn1_w_down[i], _row(ln1_g[i]), _row(ln1_b[i]))
        g2, b2 = _row(ln2_g[i]), _row(ln2_b[i])
        if kind == 0:
            win = a_w_in[j].astype(_BF16)
            wout = a_w_out[j].astype(_BF16)
            lng, lnb = _row(a_ln_g[j]), _row(a_ln_b[j])
            bias = jnp.repeat(a_b_s[j].T, HEAD_A, axis=1)
            reps = CHUNK // dec_seq
            wm_s = jnp.tile(a_w_s[j][:, :dec_seq, :dec_seq], (1, reps, reps))
            bias_s = jnp.tile(bias[:dec_seq], (reps, 1))
            xp = _mixer_a(xp, win, lng, lnb, a_w_s[j], bias, wout, g2, b2, sample=False)
            xs, vs = _mixer_a(xs, win, lng, lnb, wm_s, bias_s, wout, g2, b2, sample=True)
            chunk_v_s.append(vs.reshape(bs, dec_seq, d))
        elif kind == 1:
            w = b_w_grp[j].astype(_BF16)
            scale = _row(b_scale[j])
            hist = state_pool[j]
            pool_p.append(xp.reshape(bp, seq, d)[:, seq - POOL_HIST:, :])
            pool_s.append(jnp.concatenate(
                [hist[:, dec_seq:, :], xs.reshape(bs, dec_seq, d)], axis=1)[:, -POOL_HIST:, :])
            hist16 = _front_pad_hist(hist, 2 * SUBLANES).reshape(bs, 2, SUBLANES, d)
            h2 = hist16[:, 0].reshape(bs * SUBLANES, d)
            h1 = hist16[:, 1].reshape(bs * SUBLANES, d)
            xp = _mixer_b_prompt(xp, w, scale, g2, b2, seq=seq)
            xs = _mixer_b_sample(xs, h1, h2, w, scale, g2, b2)
        else:
            win = c_w_in[j].astype(_BF16)
            wout = c_w_out[j].astype(_BF16)
            h1 = _front_pad_hist(state_conv[j], SUBLANES)
            xp, ztail = _mixer_c_prompt(xp, win, c_w_conv[j], wout, g2, b2, seq=seq)
            xs, zs = _mixer_c_sample(xs, h1, win, c_w_conv[j], wout, g2, b2)
            tiles_per_seq = seq // MIX_TM
            ztail = ztail.reshape(bp, tiles_per_seq, SUBLANES, d)
            conv_p.append(ztail[:, -1, SUBLANES - (CONV_W - 1):, :])
            conv_s.append(zs.reshape(bs, dec_seq, d)[:, dec_seq - (CONV_W - 1):, :])
        xp = _ffn(xp, ffn2_w_gu[i], ffn2_w_down[i], _row(ln3_g[i]), _row(ln3_b[i]))
        xs = _ffn(xs, ffn2_w_gu[i], ffn2_w_down[i], _row(ln3_g[i]), _row(ln3_b[i]))
    return (xp.reshape(bp, seq, d), xs.reshape(bs, dec_seq, d),
            jnp.stack(pool_p), jnp.stack(pool_s), jnp.stack(conv_p), jnp.stack(conv_s),
            jnp.stack(chunk_v_s))
```

```python
import functools
import math

import jax
import jax.numpy as jnp
from jax import lax
from jax.experimental import pallas as pl
from jax.experimental.pallas import tpu as pltpu

D_MODEL = 2048
DEPTH = 4
PAST_LEN = 16384
D_FF = 5504
CHUNK = 128
N_GROUPS_A = 16
HEAD_A = 128
POOL_WINDOWS = (2, 4, 8, 16)
POOL_GROUP = 512
POOL_HIST = 15
CONV_W = 3
ALPHA = (2 * DEPTH) ** 0.25
LN_EPS = 1e-5

SUBLANES = 8
LANES = 128
VMEM_LIMIT_BYTES = 62 * 1024 * 1024

FFN_TM = 1024
FFN_TF = 512
FFN_RC = 256
MIX_TM = 256

_BF16 = jnp.bfloat16
_F32 = jnp.float32


def _ln(y, g, b):
    mu = jnp.mean(y, axis=-1, keepdims=True)
    d = y - mu
    var = jnp.mean(d * d, axis=-1, keepdims=True)
    return d * lax.rsqrt(var + LN_EPS) * g + b


def _dot(a, b):
    return jnp.dot(a, b, preferred_element_type=_F32)


def _resident(shape):
    zeros = (0,) * len(shape)
    return pl.BlockSpec(shape, lambda *_: zeros, pipeline_mode=pl.Buffered(1))


def _ffn_kernel(x_ref, wg_ref, wu_ref, wd_ref, g_ref, b_ref, o_ref, *, nf, tm):
    j = pl.program_id(1)

    @pl.when(j == 0)
    def _():
        o_ref[...] = jnp.zeros_like(o_ref)

    def accumulate(lo, finish):
        def up_proj(r):
            xb = x_ref[pl.ds(r, FFN_RC), :].astype(_BF16)
            return (_dot(xb, wg_ref[:, lo:].astype(_BF16)),
                    _dot(xb, wu_ref[:, lo:].astype(_BF16)))

        hu = up_proj(0)
        for r in range(0, tm, FFN_RC):
            rows = pl.ds(r, FFN_RC)
            h, u = hu
            if r + FFN_RC < tm:
                hu = up_proj(r + FFN_RC)
            a = (jax.nn.silu(h) * u).astype(_BF16)
            acc = o_ref[rows, :] + _dot(a, wd_ref[lo:, :].astype(_BF16))
            if finish:
                acc = _ln(ALPHA * x_ref[rows, :] + 0.5 * acc, g_ref[...], b_ref[...])
            o_ref[rows, :] = acc

    @pl.when(j < nf - 1)
    def _():
        accumulate(0, finish=False)

    @pl.when(j == nf - 1)
    def _():
        accumulate(nf * FFN_TF - D_FF, finish=True)


def _ff_window_start(j, base=0):
    lane_tiles = jnp.minimum(j * (FFN_TF // LANES), (D_FF - FFN_TF) // LANES)
    return (lane_tiles + base // LANES) * LANES


def _ffn(x, w_gu, w_down, layer, g, b):
    t = x.shape[0]
    tm = min(FFN_TM, t)
    nf = pl.cdiv(D_FF, FFN_TF)
    col_window = (pl.Squeezed(), pl.Element(D_MODEL), pl.Element(FFN_TF))
    row_window = (pl.Squeezed(), pl.Element(FFN_TF), pl.Element(D_MODEL))
    return pl.pallas_call(
        functools.partial(_ffn_kernel, nf=nf, tm=tm),
        grid=(t // tm, nf),
        in_specs=[
            pl.BlockSpec((tm, D_MODEL), lambda i, j: (i, 0)),
            pl.BlockSpec(col_window, lambda i, j: (layer, 0, _ff_window_start(j))),
            pl.BlockSpec(col_window, lambda i, j: (layer, 0, _ff_window_start(j, base=D_FF))),
            pl.BlockSpec(row_window, lambda i, j: (layer, _ff_window_start(j), 0)),
            pl.BlockSpec((1, D_MODEL), lambda i, j: (0, 0)),
            pl.BlockSpec((1, D_MODEL), lambda i, j: (0, 0)),
        ],
        out_specs=pl.BlockSpec((tm, D_MODEL), lambda i, j: (i, 0)),
        out_shape=jax.ShapeDtypeStruct((t, D_MODEL), _F32),
        compiler_params=pltpu.CompilerParams(
            dimension_semantics=("arbitrary", "arbitrary"),
            vmem_limit_bytes=VMEM_LIMIT_BYTES),
        name="ffn",
    )(x, w_gu, w_gu, w_down, g, b)


def _mixer_a_kernel(x_ref, win_ref, lng_ref, lnb_ref, wm_ref, bias_ref, wout_ref,
                    g_ref, b_ref, *out_and_scratch, sample, tm):
    if sample:
        o_ref, v_ref, mix_ref = out_and_scratch
    else:
        o_ref, mix_ref = out_and_scratch
    x = x_ref[...]
    xb = x.astype(_BF16)
    sqrt_half = math.sqrt(0.5)
    pre_u = _dot(xb, win_ref[:, :D_MODEL])
    u = 0.5 * pre_u * (1.0 + lax.erf(pre_u * sqrt_half))
    pre_v = _dot(xb, win_ref[:, D_MODEL:])
    v = 0.5 * pre_v * (1.0 + lax.erf(pre_v * sqrt_half))
    v = _ln(v, lng_ref[...], lnb_ref[...])
    if sample:
        v_ref[...] = v
    vb = v.astype(_BF16)

    row = lax.broadcasted_iota(jnp.int32, (CHUNK, CHUNK), 0)
    col = lax.broadcasted_iota(jnp.int32, (CHUNK, CHUNK), 1)
    mask = col <= row
    if sample:
        mask = jnp.logical_and(mask, (row // SUBLANES) == (col // SUBLANES))
    for hd in range(N_GROUPS_A):
        wm = jnp.where(mask, wm_ref[hd], 0.0).astype(_BF16)
        cols = pl.ds(hd * HEAD_A, HEAD_A)
        for c in range(tm // CHUNK):
            vv = vb[c * CHUNK:(c + 1) * CHUNK, hd * HEAD_A:(hd + 1) * HEAD_A]
            mix_ref[pl.ds(c * CHUNK, CHUNK), cols] = _dot(wm, vv)
    for c in range(tm // CHUNK):
        rows = pl.ds(c * CHUNK, CHUNK)
        mix_ref[rows, :] = u[c * CHUNK:(c + 1) * CHUNK, :] * (mix_ref[rows, :] + bias_ref[...])
    m = _dot(mix_ref[...].astype(_BF16), wout_ref[...])
    o_ref[...] = _ln(ALPHA * x + m, g_ref[...], b_ref[...])


def _mixer_a(x, win, lng, lnb, wm, bias, wout, g, b, *, sample):
    t = x.shape[0]
    tm = MIX_TM
    row_spec = pl.BlockSpec((tm, D_MODEL), lambda i: (i, 0))
    out_shape = jax.ShapeDtypeStruct((t, D_MODEL), _F32)
    return pl.pallas_call(
        functools.partial(_mixer_a_kernel, sample=sample, tm=tm),
        grid=(t // tm,),
        in_specs=[
            row_spec,
            _resident((D_MODEL, 2 * D_MODEL)),
            _resident((1, D_MODEL)),
            _resident((1, D_MODEL)),
            _resident((N_GROUPS_A, CHUNK, CHUNK)),
            _resident((CHUNK, D_MODEL)),
            _resident((D_MODEL, D_MODEL)),
            _resident((1, D_MODEL)),
            _resident((1, D_MODEL)),
        ],
        out_specs=(row_spec, row_spec) if sample else row_spec,
        out_shape=(out_shape, out_shape) if sample else out_shape,
        scratch_shapes=[pltpu.VMEM((tm, D_MODEL), _F32)],
        compiler_params=pltpu.CompilerParams(
            dimension_semantics=("arbitrary",), vmem_limit_bytes=VMEM_LIMIT_BYTES),
        name="mixer_a_sample" if sample else "mixer_a_prompt",
    )(x, win, lng, lnb, wm, bias, wout, g, b)


def _pool_project(x, win_sums, cnts, w_ref, scale_ref, g_ref, b_ref, o_ref):
    parts = []
    for gi in range(len(POOL_WINDOWS)):
        xg = x[:, gi * POOL_GROUP:(gi + 1) * POOL_GROUP]
        pooled = (win_sums[gi] / cnts[gi] - xg).astype(_BF16)
        parts.append(_dot(pooled, w_ref[gi]))
    m = jnp.concatenate(parts, axis=1) * scale_ref[...]
    o_ref[...] = _ln(ALPHA * x + m, g_ref[...], b_ref[...])


def _mixer_b_prompt_kernel(x_ref, halo_ref, w_ref, scale_ref, g_ref, b_ref, o_ref, xe_ref,
                           *, tm, tiles_per_seq):
    halo_rows = 2 * SUBLANES
    i = pl.program_id(0)
    tile_in_seq = i % tiles_per_seq
    x = x_ref[...]
    xe_ref[pl.ds(0, halo_rows), :] = jnp.where(tile_in_seq == 0, 0.0, halo_ref[...])
    xe_ref[pl.ds(halo_rows, tm), :] = x
    pos = tile_in_seq * tm + lax.broadcasted_iota(jnp.int32, (tm, 1), 0)
    sums, cnts = [], []
    for gi, w in enumerate(POOL_WINDOWS):
        cols = pl.ds(gi * POOL_GROUP, POOL_GROUP)
        s = x[:, gi * POOL_GROUP:(gi + 1) * POOL_GROUP]
        for k in range(1, w):
            s = s + xe_ref[pl.ds(halo_rows - k, tm), cols]
        sums.append(s)
        cnts.append(jnp.minimum(pos + 1, w).astype(_F32))
    _pool_project(x, sums, cnts, w_ref, scale_ref, g_ref, b_ref, o_ref)


def _mixer_b_sample_kernel(x_ref, h1_ref, h2_ref, w_ref, scale_ref, g_ref, b_ref, o_ref, *, tm):
    x = x_ref[...]
    t = lax.broadcasted_iota(jnp.int32, (tm, 1), 0) % SUBLANES
    sums, cnts = [], []
    for gi, w in enumerate(POOL_WINDOWS):
        lo, hi = gi * POOL_GROUP, (gi + 1) * POOL_GROUP
        cur, h1, h2 = x[:, lo:hi], h1_ref[:, lo:hi], h2_ref[:, lo:hi]
        s = cur
        for k in range(1, w):
            kk = k % SUBLANES
            new, old = (cur, h1) if k < SUBLANES else (h1, h2)
            if kk == 0:
                s = s + new
            else:
                s = s + jnp.where(t >= kk, pltpu.roll(new, kk, 0),
                                  pltpu.roll(old, tm + kk - SUBLANES, 0))
        sums.append(s)
        cnts.append(jnp.minimum(PAST_LEN + t + 1, w).astype(_F32))
    _pool_project(x, sums, cnts, w_ref, scale_ref, g_ref, b_ref, o_ref)


def _mixer_b_prompt(x, w, scale, g, b, *, seq):
    t = x.shape[0]
    tm = MIX_TM
    halo_rows = 2 * SUBLANES
    ratio = tm // halo_rows
    return pl.pallas_call(
        functools.partial(_mixer_b_prompt_kernel, tm=tm, tiles_per_seq=seq // tm),
        grid=(t // tm,),
        in_specs=[
            pl.BlockSpec((tm, D_MODEL), lambda i: (i, 0)),
            pl.BlockSpec((halo_rows, D_MODEL), lambda i: (jnp.maximum(i * ratio - 1, 0), 0)),
            _resident((len(POOL_WINDOWS), POOL_GROUP, POOL_GROUP)),
            _resident((1, D_MODEL)),
            _resident((1, D_MODEL)),
            _resident((1, D_MODEL)),
        ],
        out_specs=pl.BlockSpec((tm, D_MODEL), lambda i: (i, 0)),
        out_shape=jax.ShapeDtypeStruct((t, D_MODEL), _F32),
        scratch_shapes=[pltpu.VMEM((tm + halo_rows, D_MODEL), _F32)],
        compiler_params=pltpu.CompilerParams(
            dimension_semantics=("arbitrary",), vmem_limit_bytes=VMEM_LIMIT_BYTES),
        name="mixer_b_prompt",
    )(x, x, w, scale, g, b)


def _mixer_b_sample(x, h1, h2, w, scale, g, b):
    t = x.shape[0]
    tm = MIX_TM
    row_spec = pl.BlockSpec((tm, D_MODEL), lambda i: (i, 0))
    return pl.pallas_call(
        functools.partial(_mixer_b_sample_kernel, tm=tm),
        grid=(t // tm,),
        in_specs=[
            row_spec, row_spec, row_spec,
            _resident((len(POOL_WINDOWS), POOL_GROUP, POOL_GROUP)),
            _resident((1, D_MODEL)),
            _resident((1, D_MODEL)),
            _resident((1, D_MODEL)),
        ],
        out_specs=row_spec,
        out_shape=jax.ShapeDtypeStruct((t, D_MODEL), _F32),
        compiler_params=pltpu.CompilerParams(
            dimension_semantics=("arbitrary",), vmem_limit_bytes=VMEM_LIMIT_BYTES),
        name="mixer_b_sample",
    )(x, h1, h2, w, scale, g, b)


def _conv_project(x, xb, z, z1, z2, win_ref, wconv_ref, wout_ref, g_ref, b_ref, o_ref):
    conv = wconv_ref[2:3, :] * z + wconv_ref[0:1, :] * z2 + wconv_ref[1:2, :] * z1
    bq = _dot(xb, win_ref[:, :D_MODEL])
    m = _dot((bq * conv).astype(_BF16), wout_ref[...])
    o_ref[...] = _ln(ALPHA * x + m, g_ref[...], b_ref[...])


def _gate_product(xb, win_ref):
    c = _dot(xb, win_ref[:, D_MODEL:2 * D_MODEL])
    h = _dot(xb, win_ref[:, 2 * D_MODEL:])
    return c * h


def _mixer_c_prompt_kernel(x_ref, win_ref, wconv_ref, wout_ref, g_ref, b_ref,
                           o_ref, ztail_ref, ze_ref, carry_ref, *, tm, tiles_per_seq):
    i = pl.program_id(0)
    x = x_ref[...]
    xb = x.astype(_BF16)
    z = _gate_product(xb, win_ref)
    first = (i % tiles_per_seq) == 0
    ze_ref[pl.ds(0, SUBLANES), :] = jnp.where(first, 0.0, carry_ref[...])
    ze_ref[pl.ds(SUBLANES, tm), :] = z
    tail = z[tm - SUBLANES:, :]
    carry_ref[...] = tail
    ztail_ref[...] = tail
    z1 = ze_ref[pl.ds(SUBLANES - 1, tm), :]
    z2 = ze_ref[pl.ds(SUBLANES - 2, tm), :]
    _conv_project(x, xb, z, z1, z2, win_ref, wconv_ref, wout_ref, g_ref, b_ref, o_ref)


def _mixer_c_sample_kernel(x_ref, h1_ref, win_ref, wconv_ref, wout_ref, g_ref, b_ref,
                           o_ref, z_ref, *, tm):
    x = x_ref[...]
    xb = x.astype(_BF16)
    z = _gate_product(xb, win_ref)
    z_ref[...] = z
    t = lax.broadcasted_iota(jnp.int32, (tm, 1), 0) % SUBLANES
    h1 = h1_ref[...]
    z1 = jnp.where(t >= 1, pltpu.roll(z, 1, 0), pltpu.roll(h1, tm + 1 - SUBLANES, 0))
    z2 = jnp.where(t >= 2, pltpu.roll(z, 2, 0), pltpu.roll(h1, tm + 2 - SUBLANES, 0))
    _conv_project(x, xb, z, z1, z2, win_ref, wconv_ref, wout_ref, g_ref, b_ref, o_ref)


def _mixer_c_specs():
    return [
        _resident((D_MODEL, 3 * D_MODEL)),
        _resident((CONV_W, D_MODEL)),
        _resident((D_MODEL, D_MODEL)),
        _resident((1, D_MODEL)),
        _resident((1, D_MODEL)),
    ]


def _mixer_c_prompt(x, win, wconv, wout, g, b, *, seq):
    t = x.shape[0]
    tm = MIX_TM
    row_spec = pl.BlockSpec((tm, D_MODEL), lambda i: (i, 0))
    return pl.pallas_call(
        functools.partial(_mixer_c_prompt_kernel, tm=tm, tiles_per_seq=seq // tm),
        grid=(t // tm,),
        in_specs=[row_spec] + _mixer_c_specs(),
        out_specs=(row_spec, pl.BlockSpec((SUBLANES, D_MODEL), lambda i: (i, 0))),
        out_shape=(jax.ShapeDtypeStruct((t, D_MODEL), _F32),
                   jax.ShapeDtypeStruct((t // tm * SUBLANES, D_MODEL), _F32)),
        scratch_shapes=[pltpu.VMEM((tm + SUBLANES, D_MODEL), _F32),
                        pltpu.VMEM((SUBLANES, D_MODEL), _F32)],
        compiler_params=pltpu.CompilerParams(
            dimension_semantics=("arbitrary",), vmem_limit_bytes=VMEM_LIMIT_BYTES),
        name="mixer_c_prompt",
    )(x, win, wconv, wout, g, b)


def _mixer_c_sample(x, h1, win, wconv, wout, g, b):
    t = x.shape[0]
    tm = MIX_TM
    row_spec = pl.BlockSpec((tm, D_MODEL), lambda i: (i, 0))
    out_shape = jax.ShapeDtypeStruct((t, D_MODEL), _F32)
    return pl.pallas_call(
        functools.partial(_mixer_c_sample_kernel, tm=tm),
        grid=(t // tm,),
        in_specs=[row_spec, row_spec] + _mixer_c_specs(),
        out_specs=(row_spec, row_spec),
        out_shape=(out_shape, out_shape),
        compiler_params=pltpu.CompilerParams(
            dimension_semantics=("arbitrary",), vmem_limit_bytes=VMEM_LIMIT_BYTES),
        name="mixer_c_sample",
    )(x, h1, win, wconv, wout, g, b)


def _row(v):
    return v.reshape(1, -1)


def _front_pad_hist(hist, rows):
    bsz, h, d = hist.shape
    return jnp.pad(hist, ((0, 0), (rows - h, 0), (0, 0))).reshape(bsz * rows, d)


def kernel(x_prompt, x_sample, state_pool, state_conv, ffn1_w_gu, ffn1_w_down, ffn2_w_gu, ffn2_w_down, ln1_g, ln1_b, ln2_g, ln2_b, ln3_g, ln3_b, a_w_in, a_ln_g, a_ln_b, a_w_s, a_b_s, a_w_out, b_w_grp, b_scale, c_w_in, c_w_conv, c_w_out):
    bp, seq, d = x_prompt.shape
    bs, dec_seq, _ = x_sample.shape
    assert dec_seq == SUBLANES and seq % MIX_TM == 0 and (bs * dec_seq) % MIX_TM == 0
    xp = x_prompt.reshape(bp * seq, d)
    xs = x_sample.reshape(bs * dec_seq, d)
    pool_p, pool_s, conv_p, conv_s, chunk_v_s = [], [], [], [], []
    for i in range(DEPTH):
        kind, j = i % 3, i // 3
        xp = _ffn(xp, ffn1_w_gu, ffn1_w_down, i, _row(ln1_g[i]), _row(ln1_b[i]))
        xs = _ffn(xs, ffn1_w_gu, ffn1_w_down, i, _row(ln1_g[i]), _row(ln1_b[i]))
        g2, b2 = _row(ln2_g[i]), _row(ln2_b[i])
        if kind == 0:
            win = a_w_in[j].astype(_BF16)
            wout = a_w_out[j].astype(_BF16)
            lng, lnb = _row(a_ln_g[j]), _row(a_ln_b[j])
            bias = jnp.repeat(a_b_s[j].T, HEAD_A, axis=1)
            reps = CHUNK // dec_seq
            wm_s = jnp.tile(a_w_s[j][:, :dec_seq, :dec_seq], (1, reps, reps))
            bias_s = jnp.tile(bias[:dec_seq], (reps, 1))
            xp = _mixer_a(xp, win, lng, lnb, a_w_s[j], bias, wout, g2, b2, sample=False)
            xs, vs = _mixer_a(xs, win, lng, lnb, wm_s, bias_s, wout, g2, b2, sample=True)
            chunk_v_s.append(vs.reshape(bs, dec_seq, d))
        elif kind == 1:
            w = b_w_grp[j].astype(_BF16)
            scale = _row(b_scale[j])
            hist = state_pool[j]
            pool_p.append(xp.reshape(bp, seq, d)[:, seq - POOL_HIST:, :])
            pool_s.append(jnp.concatenate(
                [hist[:, dec_seq:, :], xs.reshape(bs, dec_seq, d)], axis=1)[:, -POOL_HIST:, :])
            hist16 = _front_pad_hist(hist, 2 * SUBLANES).reshape(bs, 2, SUBLANES, d)
            h2 = hist16[:, 0].reshape(bs * SUBLANES, d)
            h1 = hist16[:, 1].reshape(bs * SUBLANES, d)
            xp = _mixer_b_prompt(xp, w, scale, g2, b2, seq=seq)
            xs = _mixer_b_sample(xs, h1, h2, w, scale, g2, b2)
        else:
            win = c_w_in[j].astype(_BF16)
            wout = c_w_out[j].astype(_BF16)
            h1 = _front_pad_hist(state_conv[j], SUBLANES)
            xp, ztail = _mixer_c_prompt(xp, win, c_w_conv[j], wout, g2, b2, seq=seq)
            xs, zs = _mixer_c_sample(xs, h1, win, c_w_conv[j], wout, g2, b2)
            tiles_per_seq = seq // MIX_TM
            ztail = ztail.reshape(bp, tiles_per_seq, SUBLANES, d)
            conv_p.append(ztail[:, -1, SUBLANES - (CONV_W - 1):, :])
            conv_s.append(zs.reshape(bs, dec_seq, d)[:, dec_seq - (CONV_W - 1):, :])
        xp = _ffn(xp, ffn2_w_gu, ffn2_w_down, i, _row(ln3_g[i]), _row(ln3_b[i]))
        xs = _ffn(xs, ffn2_w_gu, ffn2_w_down, i, _row(ln3_g[i]), _row(ln3_b[i]))
    return (xp.reshape(bp, seq, d), xs.reshape(bs, dec_seq, d),
            jnp.stack(pool_p), jnp.stack(pool_s), jnp.stack(conv_p), jnp.stack(conv_s),
            jnp.stack(chunk_v_s))
```

```python
import functools
import math

import jax
import jax.numpy as jnp
from jax import lax
from jax.experimental import pallas as pl
from jax.experimental.pallas import tpu as pltpu

D_MODEL = 2048
DEPTH = 4
PAST_LEN = 16384
D_FF = 5504
CHUNK = 128
N_GROUPS_A = 16
HEAD_A = 128
POOL_WINDOWS = (2, 4, 8, 16)
POOL_GROUP = 512
POOL_HIST = 15
CONV_W = 3
ALPHA = (2 * DEPTH) ** 0.25
LN_EPS = 1e-5

SUBLANES = 8
LANES = 128
MXU_TILE = 256
VMEM_LIMIT_BYTES = 62 * 1024 * 1024
MIXER_VMEM_LIMIT_BYTES = 56 * 1024 * 1024

FFN_TM = 1024
FFN_TF = 512
FFN_RC = 256
MIX_TM = 256
MIX_A_TM = 256

_BF16 = jnp.bfloat16
_F32 = jnp.float32


def _ln(y, g, b):
    mu = jnp.mean(y, axis=-1, keepdims=True)
    d = y - mu
    var = jnp.mean(d * d, axis=-1, keepdims=True)
    return d * lax.rsqrt(var + LN_EPS) * g + b


def _dot(a, b):
    return jnp.dot(a, b, preferred_element_type=_F32)


def _resident(shape):
    zeros = (0,) * len(shape)
    return pl.BlockSpec(shape, lambda *_: zeros, pipeline_mode=pl.Buffered(1))


def _ffn_kernel(x_ref, wg_ref, wu_ref, wd_ref, g_ref, b_ref, o_ref, *, nf, tm):
    j = pl.program_id(1)

    @pl.when(j == 0)
    def _():
        o_ref[...] = (2.0 * ALPHA) * x_ref[...]

    def accumulate(lo, finish):
        mid = lo + (FFN_TF - lo) // MXU_TILE * MXU_TILE
        rem = FFN_TF - mid
        assert rem in (0, MXU_TILE // 2)

        def up_proj(r):
            xb = x_ref[pl.ds(r, FFN_RC), :].astype(_BF16)
            h = [_dot(xb, wg_ref[:, lo:mid].astype(_BF16))]
            u = [_dot(xb, wu_ref[:, lo:mid].astype(_BF16))]
            if rem:
                w_rem = jnp.concatenate([wg_ref[:, mid:].astype(_BF16),
                                         wu_ref[:, mid:].astype(_BF16)], axis=1)
                hu_rem = _dot(xb, w_rem)
                h.append(hu_rem[:, :rem])
                u.append(hu_rem[:, rem:])
            return h, u

        hu = up_proj(0)
        for r in range(0, tm, FFN_RC):
            rows = pl.ds(r, FFN_RC)
            h, u = hu
            if r + FFN_RC < tm:
                hu = up_proj(r + FFN_RC)
            acc = o_ref[rows, :]
            for hk, uk, (k0, k1) in zip(h, u, ((lo, mid), (mid, FFN_TF))):
                a = (jax.nn.silu(hk) * uk).astype(_BF16)
                acc = acc + _dot(a, wd_ref[k0:k1, :].astype(_BF16))
            if finish:
                acc = _ln(0.5 * acc, g_ref[...], b_ref[...])
            o_ref[rows, :] = acc

    @pl.when(j < nf - 1)
    def _():
        accumulate(0, finish=False)

    @pl.when(j == nf - 1)
    def _():
        accumulate(nf * FFN_TF - D_FF, finish=True)


def _ff_window_start(j, base=0):
    lane_tiles = jnp.minimum(j * (FFN_TF // LANES), (D_FF - FFN_TF) // LANES)
    return (lane_tiles + base // LANES) * LANES


def _ffn(x, w_gu, w_down, layer, g, b):
    t = x.shape[0]
    tm = min(FFN_TM, t)
    nf = pl.cdiv(D_FF, FFN_TF)
    col_window = (pl.Squeezed(), pl.Element(D_MODEL), pl.Element(FFN_TF))
    row_window = (pl.Squeezed(), pl.Element(FFN_TF), pl.Element(D_MODEL))
    return pl.pallas_call(
        functools.partial(_ffn_kernel, nf=nf, tm=tm),
        grid=(t // tm, nf),
        in_specs=[
            pl.BlockSpec((tm, D_MODEL), lambda i, j: (i, 0)),
            pl.BlockSpec(col_window, lambda i, j: (layer, 0, _ff_window_start(j))),
            pl.BlockSpec(col_window, lambda i, j: (layer, 0, _ff_window_start(j, base=D_FF))),
            pl.BlockSpec(row_window, lambda i, j: (layer, _ff_window_start(j), 0)),
            pl.BlockSpec((1, D_MODEL), lambda i, j: (0, 0)),
            pl.BlockSpec((1, D_MODEL), lambda i, j: (0, 0)),
        ],
        out_specs=pl.BlockSpec((tm, D_MODEL), lambda i, j: (i, 0)),
        out_shape=jax.ShapeDtypeStruct((t, D_MODEL), _F32),
        compiler_params=pltpu.CompilerParams(
            dimension_semantics=("arbitrary", "arbitrary"),
            vmem_limit_bytes=VMEM_LIMIT_BYTES),
        name="ffn",
    )(x, w_gu, w_gu, w_down, g, b)


def _mixer_a_kernel(x_ref, win_ref, lng_ref, lnb_ref, wm_ref, bias_ref, wout_ref,
                    g_ref, b_ref, *out_and_scratch, sample, tm):
    if sample:
        o_ref, v_ref, mix_ref = out_and_scratch
    else:
        o_ref, mix_ref = out_and_scratch
    sqrt_half = math.sqrt(0.5)

    def gelu(z):
        return 0.5 * z * (1.0 + lax.erf(z * sqrt_half))

    def in_proj(c):
        xb = x_ref[pl.ds(c * CHUNK, CHUNK), :].astype(_BF16)
        return _dot(xb, win_ref[:, :D_MODEL]), _dot(xb, win_ref[:, D_MODEL:])

    row = lax.broadcasted_iota(jnp.int32, (CHUNK, CHUNK), 0)
    col = lax.broadcasted_iota(jnp.int32, (CHUNK, CHUNK), 1)
    mask = col <= row
    if sample:
        mask = jnp.logical_and(mask, (row // SUBLANES) == (col // SUBLANES))

    pre = in_proj(0)
    for c in range(tm // CHUNK):
        rows = pl.ds(c * CHUNK, CHUNK)
        pre_u, pre_v = pre
        if c + 1 < tm // CHUNK:
            pre = in_proj(c + 1)
        u = gelu(pre_u)
        v = _ln(gelu(pre_v), lng_ref[...], lnb_ref[...])
        if sample:
            v_ref[rows, :] = v
        vb = v.astype(_BF16)
        for hd in range(N_GROUPS_A):
            wm = jnp.where(mask, wm_ref[hd], 0.0).astype(_BF16)
            cols = pl.ds(hd * HEAD_A, HEAD_A)
            mix_ref[rows, cols] = _dot(wm, vb[:, hd * HEAD_A:(hd + 1) * HEAD_A])
        gated = (u * (mix_ref[rows, :] + bias_ref[...])).astype(_BF16)
        m = _dot(gated, wout_ref[...])
        o_ref[rows, :] = _ln(ALPHA * x_ref[rows, :] + m, g_ref[...], b_ref[...])


def _mixer_a(x, win, lng, lnb, wm, bias, wout, g, b, *, sample):
    t = x.shape[0]
    tm = MIX_A_TM
    row_spec = pl.BlockSpec((tm, D_MODEL), lambda i: (i, 0))
    out_shape = jax.ShapeDtypeStruct((t, D_MODEL), _F32)
    return pl.pallas_call(
        functools.partial(_mixer_a_kernel, sample=sample, tm=tm),
        grid=(t // tm,),
        in_specs=[
            row_spec,
            _resident((D_MODEL, 2 * D_MODEL)),
            _resident((1, D_MODEL)),
            _resident((1, D_MODEL)),
            _resident((N_GROUPS_A, CHUNK, CHUNK)),
            _resident((CHUNK, D_MODEL)),
            _resident((D_MODEL, D_MODEL)),
            _resident((1, D_MODEL)),
            _resident((1, D_MODEL)),
        ],
        out_specs=(row_spec, row_spec) if sample else row_spec,
        out_shape=(out_shape, out_shape) if sample else out_shape,
        scratch_shapes=[pltpu.VMEM((tm, D_MODEL), _F32)],
        compiler_params=pltpu.CompilerParams(
            dimension_semantics=("arbitrary",), vmem_limit_bytes=MIXER_VMEM_LIMIT_BYTES),
        name="mixer_a_sample" if sample else "mixer_a_prompt",
    )(x, win, lng, lnb, wm, bias, wout, g, b)


def _pool_project(x, win_sums, cnts, w_ref, scale_ref, g_ref, b_ref, o_ref):
    parts = []
    for gi in range(len(POOL_WINDOWS)):
        xg = x[:, gi * POOL_GROUP:(gi + 1) * POOL_GROUP]
        pooled = (win_sums[gi] / cnts[gi] - xg).astype(_BF16)
        parts.append(_dot(pooled, w_ref[gi]))
    m = jnp.concatenate(parts, axis=1) * scale_ref[...]
    o_ref[...] = _ln(ALPHA * x + m, g_ref[...], b_ref[...])


def _mixer_b_prompt_kernel(x_ref, halo_ref, w_ref, scale_ref, g_ref, b_ref, o_ref, xe_ref,
                           *, tm, tiles_per_seq):
    halo_rows = 2 * SUBLANES
    i = pl.program_id(0)
    tile_in_seq = i % tiles_per_seq
    x = x_ref[...]
    xe_ref[pl.ds(0, halo_rows), :] = jnp.where(tile_in_seq == 0, 0.0, halo_ref[...])
    xe_ref[pl.ds(halo_rows, tm), :] = x
    pos = tile_in_seq * tm + lax.broadcasted_iota(jnp.int32, (tm, 1), 0)
    sums, cnts = [], []
    for gi, w in enumerate(POOL_WINDOWS):
        cols = pl.ds(gi * POOL_GROUP, POOL_GROUP)
        s = x[:, gi * POOL_GROUP:(gi + 1) * POOL_GROUP]
        for k in range(1, w):
            s = s + xe_ref[pl.ds(halo_rows - k, tm), cols]
        sums.append(s)
        cnts.append(jnp.minimum(pos + 1, w).astype(_F32))
    _pool_project(x, sums, cnts, w_ref, scale_ref, g_ref, b_ref, o_ref)


def _mixer_b_sample_kernel(x_ref, h1_ref, h2_ref, w_ref, scale_ref, g_ref, b_ref, o_ref, *, tm):
    x = x_ref[...]
    t = lax.broadcasted_iota(jnp.int32, (tm, 1), 0) % SUBLANES
    sums, cnts = [], []
    for gi, w in enumerate(POOL_WINDOWS):
        lo, hi = gi * POOL_GROUP, (gi + 1) * POOL_GROUP
        cur, h1, h2 = x[:, lo:hi], h1_ref[:, lo:hi], h2_ref[:, lo:hi]
        s = cur
        for k in range(1, w):
            kk = k % SUBLANES
            new, old = (cur, h1) if k < SUBLANES else (h1, h2)
            if kk == 0:
                s = s + new
            else:
                s = s + jnp.where(t >= kk, pltpu.roll(new, kk, 0),
                                  pltpu.roll(old, tm + kk - SUBLANES, 0))
        sums.append(s)
        cnts.append(jnp.minimum(PAST_LEN + t + 1, w).astype(_F32))
    _pool_project(x, sums, cnts, w_ref, scale_ref, g_ref, b_ref, o_ref)


def _mixer_b_prompt(x, w, scale, g, b, *, seq):
    t = x.shape[0]
    tm = MIX_TM
    halo_rows = 2 * SUBLANES
    ratio = tm // halo_rows
    return pl.pallas_call(
        functools.partial(_mixer_b_prompt_kernel, tm=tm, tiles_per_seq=seq // tm),
        grid=(t // tm,),
        in_specs=[
            pl.BlockSpec((tm, D_MODEL), lambda i: (i, 0)),
            pl.BlockSpec((halo_rows, D_MODEL), lambda i: (jnp.maximum(i * ratio - 1, 0), 0)),
            _resident((len(POOL_WINDOWS), POOL_GROUP, POOL_GROUP)),
            _resident((1, D_MODEL)),
            _resident((1, D_MODEL)),
            _resident((1, D_MODEL)),
        ],
        out_specs=pl.BlockSpec((tm, D_MODEL), lambda i: (i, 0)),
        out_shape=jax.ShapeDtypeStruct((t, D_MODEL), _F32),
        scratch_shapes=[pltpu.VMEM((tm + halo_rows, D_MODEL), _F32)],
        compiler_params=pltpu.CompilerParams(
            dimension_semantics=("arbitrary",), vmem_limit_bytes=MIXER_VMEM_LIMIT_BYTES),
        name="mixer_b_prompt",
    )(x, x, w, scale, g, b)


def _mixer_b_sample(x, h1, h2, w, scale, g, b):
    t = x.shape[0]
    tm = MIX_TM
    row_spec = pl.BlockSpec((tm, D_MODEL), lambda i: (i, 0))
    return pl.pallas_call(
        functools.partial(_mixer_b_sample_kernel, tm=tm),
        grid=(t // tm,),
        in_specs=[
            row_spec, row_spec, row_spec,
            _resident((len(POOL_WINDOWS), POOL_GROUP, POOL_GROUP)),
            _resident((1, D_MODEL)),
            _resident((1, D_MODEL)),
            _resident((1, D_MODEL)),
        ],
        out_specs=row_spec,
        out_shape=jax.ShapeDtypeStruct((t, D_MODEL), _F32),
        compiler_params=pltpu.CompilerParams(
            dimension_semantics=("arbitrary",), vmem_limit_bytes=MIXER_VMEM_LIMIT_BYTES),
        name="mixer_b_sample",
    )(x, h1, h2, w, scale, g, b)


def _conv_project(x, xb, z, z1, z2, win_ref, wconv_ref, wout_ref, g_ref, b_ref, o_ref):
    conv = wconv_ref[2:3, :] * z + wconv_ref[0:1, :] * z2 + wconv_ref[1:2, :] * z1
    bq = _dot(xb, win_ref[:, :D_MODEL])
    m = _dot((bq * conv).astype(_BF16), wout_ref[...])
    o_ref[...] = _ln(ALPHA * x + m, g_ref[...], b_ref[...])


def _gate_product(xb, win_ref):
    c = _dot(xb, win_ref[:, D_MODEL:2 * D_MODEL])
    h = _dot(xb, win_ref[:, 2 * D_MODEL:])
    return c * h


def _mixer_c_prompt_kernel(x_ref, win_ref, wconv_ref, wout_ref, g_ref, b_ref,
                           o_ref, ztail_ref, ze_ref, carry_ref, *, tm, tiles_per_seq):
    i = pl.program_id(0)
    x = x_ref[...]
    xb = x.astype(_BF16)
    z = _gate_product(xb, win_ref)
    first = (i % tiles_per_seq) == 0
    ze_ref[pl.ds(0, SUBLANES), :] = jnp.where(first, 0.0, carry_ref[...])
    ze_ref[pl.ds(SUBLANES, tm), :] = z
    tail = z[tm - SUBLANES:, :]
    carry_ref[...] = tail
    ztail_ref[...] = tail
    z1 = ze_ref[pl.ds(SUBLANES - 1, tm), :]
    z2 = ze_ref[pl.ds(SUBLANES - 2, tm), :]
    _conv_project(x, xb, z, z1, z2, win_ref, wconv_ref, wout_ref, g_ref, b_ref, o_ref)


def _mixer_c_sample_kernel(x_ref, h1_ref, win_ref, wconv_ref, wout_ref, g_ref, b_ref,
                           o_ref, z_ref, *, tm):
    x = x_ref[...]
    xb = x.astype(_BF16)
    z = _gate_product(xb, win_ref)
    z_ref[...] = z
    t = lax.broadcasted_iota(jnp.int32, (tm, 1), 0) % SUBLANES
    h1 = h1_ref[...]
    z1 = jnp.where(t >= 1, pltpu.roll(z, 1, 0), pltpu.roll(h1, tm + 1 - SUBLANES, 0))
    z2 = jnp.where(t >= 2, pltpu.roll(z, 2, 0), pltpu.roll(h1, tm + 2 - SUBLANES, 0))
    _conv_project(x, xb, z, z1, z2, win_ref, wconv_ref, wout_ref, g_ref, b_ref, o_ref)


def _mixer_c_specs():
    return [
        _resident((D_MODEL, 3 * D_MODEL)),
        _resident((CONV_W, D_MODEL)),
        _resident((D_MODEL, D_MODEL)),
        _resident((1, D_MODEL)),
        _resident((1, D_MODEL)),
    ]


def _mixer_c_prompt(x, win, wconv, wout, g, b, *, seq):
    t = x.shape[0]
    tm = MIX_TM
    row_spec = pl.BlockSpec((tm, D_MODEL), lambda i: (i, 0))
    return pl.pallas_call(
        functools.partial(_mixer_c_prompt_kernel, tm=tm, tiles_per_seq=seq // tm),
        grid=(t // tm,),
        in_specs=[row_spec] + _mixer_c_specs(),
        out_specs=(row_spec, pl.BlockSpec((SUBLANES, D_MODEL), lambda i: (i, 0))),
        out_shape=(jax.ShapeDtypeStruct((t, D_MODEL), _F32),
                   jax.ShapeDtypeStruct((t // tm * SUBLANES, D_MODEL), _F32)),
        scratch_shapes=[pltpu.VMEM((tm + SUBLANES, D_MODEL), _F32),
                        pltpu.VMEM((SUBLANES, D_MODEL), _F32)],
        compiler_params=pltpu.CompilerParams(
            dimension_semantics=("arbitrary",), vmem_limit_bytes=MIXER_VMEM_LIMIT_BYTES),
        name="mixer_c_prompt",
    )(x, win, wconv, wout, g, b)


def _mixer_c_sample(x, h1, win, wconv, wout, g, b):
    t = x.shape[0]
    tm = MIX_TM
    row_spec = pl.BlockSpec((tm, D_MODEL), lambda i: (i, 0))
    out_shape = jax.ShapeDtypeStruct((t, D_MODEL), _F32)
    return pl.pallas_call(
        functools.partial(_mixer_c_sample_kernel, tm=tm),
        grid=(t // tm,),
        in_specs=[row_spec, row_spec] + _mixer_c_specs(),
        out_specs=(row_spec, row_spec),
        out_shape=(out_shape, out_shape),
        compiler_params=pltpu.CompilerParams(
            dimension_semantics=("arbitrary",), vmem_limit_bytes=MIXER_VMEM_LIMIT_BYTES),
        name="mixer_c_sample",
    )(x, h1, win, wconv, wout, g, b)


def _row(v):
    return v.reshape(1, -1)


def _front_pad_hist(hist, rows):
    bsz, h, d = hist.shape
    return jnp.pad(hist, ((0, 0), (rows - h, 0), (0, 0))).reshape(bsz * rows, d)


def kernel(x_prompt, x_sample, state_pool, state_conv, ffn1_w_gu, ffn1_w_down, ffn2_w_gu, ffn2_w_down, ln1_g, ln1_b, ln2_g, ln2_b, ln3_g, ln3_b, a_w_in, a_ln_g, a_ln_b, a_w_s, a_b_s, a_w_out, b_w_grp, b_scale, c_w_in, c_w_conv, c_w_out):
    bp, seq, d = x_prompt.shape
    bs, dec_seq, _ = x_sample.shape
    assert dec_seq == SUBLANES
    assert all(rows % tm == 0 for rows in (seq, bs * dec_seq) for tm in (MIX_TM, MIX_A_TM))
    xp = x_prompt.reshape(bp * seq, d)
    xs = x_sample.reshape(bs * dec_seq, d)
    pool_p, pool_s, conv_p, conv_s, chunk_v_s = [], [], [], [], []
    for i in range(DEPTH):
        kind, j = i % 3, i // 3
        xp = _ffn(xp, ffn1_w_gu, ffn1_w_down, i, _row(ln1_g[i]), _row(ln1_b[i]))
        xs = _ffn(xs, ffn1_w_gu, ffn1_w_down, i, _row(ln1_g[i]), _row(ln1_b[i]))
        g2, b2 = _row(ln2_g[i]), _row(ln2_b[i])
        if kind == 0:
            win = a_w_in[j].astype(_BF16)
            wout = a_w_out[j].astype(_BF16)
            lng, lnb = _row(a_ln_g[j]), _row(a_ln_b[j])
            bias = jnp.repeat(a_b_s[j].T, HEAD_A, axis=1)
            reps = CHUNK // dec_seq
            wm_s = jnp.tile(a_w_s[j][:, :dec_seq, :dec_seq], (1, reps, reps))
            bias_s = jnp.tile(bias[:dec_seq], (reps, 1))
            xp = _mixer_a(xp, win, lng, lnb, a_w_s[j], bias, wout, g2, b2, sample=False)
            xs, vs = _mixer_a(xs, win, lng, lnb, wm_s, bias_s, wout, g2, b2, sample=True)
            chunk_v_s.append(vs.reshape(bs, dec_seq, d))
        elif kind == 1:
            w = b_w_grp[j].astype(_BF16)
            scale = _row(b_scale[j])
            hist = state_pool[j]
            pool_p.append(xp.reshape(bp, seq, d)[:, seq - POOL_HIST:, :])
            pool_s.append(jnp.concatenate(
                [hist[:, dec_seq:, :], xs.reshape(bs, dec_seq, d)], axis=1)[:, -POOL_HIST:, :])
            hist16 = _front_pad_hist(hist, 2 * SUBLANES).reshape(bs, 2, SUBLANES, d)
            h2 = hist16[:, 0].reshape(bs * SUBLANES, d)
            h1 = hist16[:, 1].reshape(bs * SUBLANES, d)
            xp = _mixer_b_prompt(xp, w, scale, g2, b2, seq=seq)
            xs = _mixer_b_sample(xs, h1, h2, w, scale, g2, b2)
        else:
            win = c_w_in[j].astype(_BF16)
            wout = c_w_out[j].astype(_BF16)
            h1 = _front_pad_hist(state_conv[j], SUBLANES)
            xp, ztail = _mixer_c_prompt(xp, win, c_w_conv[j], wout, g2, b2, seq=seq)
            xs, zs = _mixer_c_sample(xs, h1, win, c_w_conv[j], wout, g2, b2)
            tiles_per_seq = seq // MIX_TM
            ztail = ztail.reshape(bp, tiles_per_seq, SUBLANES, d)
            conv_p.append(ztail[:, -1, SUBLANES - (CONV_W - 1):, :])
            conv_s.append(zs.reshape(bs, dec_seq, d)[:, dec_seq - (CONV_W - 1):, :])
        xp = _ffn(xp, ffn2_w_gu, ffn2_w_down, i, _row(ln3_g[i]), _row(ln3_b[i]))
        xs = _ffn(xs, ffn2_w_gu, ffn2_w_down, i, _row(ln3_g[i]), _row(ln3_b[i]))
    return (xp.reshape(bp, seq, d), xs.reshape(bs, dec_seq, d),
            jnp.stack(pool_p), jnp.stack(pool_s), jnp.stack(conv_p), jnp.stack(conv_s),
            jnp.stack(chunk_v_s))
```

```python
import functools
import math

import jax
import jax.numpy as jnp
from jax import lax
from jax.experimental import pallas as pl
from jax.experimental.pallas import tpu as pltpu

D_MODEL = 2048
DEPTH = 4
PAST_LEN = 16384
D_FF = 5504
CHUNK = 128
N_GROUPS_A = 16
HEAD_A = 128
POOL_WINDOWS = (2, 4, 8, 16)
POOL_GROUP = 512
POOL_HIST = 15
CONV_W = 3
ALPHA = (2 * DEPTH) ** 0.25
LN_EPS = 1e-5

SUBLANES = 8
LANES = 128
MXU_TILE = 256
VMEM_LIMIT_BYTES = 62 * 1024 * 1024
MIXER_VMEM_LIMIT_BYTES = 56 * 1024 * 1024

FFN_TM = 1024
FFN_TF = 512
FFN_RC = 256
MIX_TM = 256
MIX_A_TM = 256

_BF16 = jnp.bfloat16
_F32 = jnp.float32


def _ln(y, g, b):
    mu = jnp.mean(y, axis=-1, keepdims=True)
    d = y - mu
    var = jnp.mean(d * d, axis=-1, keepdims=True)
    return d * lax.rsqrt(var + LN_EPS) * g + b


def _dot(a, b):
    return jnp.dot(a, b, preferred_element_type=_F32)


def _resident(shape):
    zeros = (0,) * len(shape)
    return pl.BlockSpec(shape, lambda *_: zeros, pipeline_mode=pl.Buffered(1))


def _ffn_kernel(x_ref, wg_ref, wu_ref, wd_ref, g_ref, b_ref, o_ref, *, nf, tm):
    j = pl.program_id(1)

    @pl.when(j == 0)
    def _():
        o_ref[...] = (2.0 * ALPHA) * x_ref[...]

    def accumulate(lo, finish):
        mid = lo + (FFN_TF - lo) // MXU_TILE * MXU_TILE
        rem = FFN_TF - mid
        assert rem in (0, MXU_TILE // 2)

        def up_proj(r):
            xb = x_ref[pl.ds(r, FFN_RC), :].astype(_BF16)
            h = [_dot(xb, wg_ref[:, lo:mid].astype(_BF16))]
            u = [_dot(xb, wu_ref[:, lo:mid].astype(_BF16))]
            if rem:
                w_rem = jnp.concatenate([wg_ref[:, mid:].astype(_BF16),
                                         wu_ref[:, mid:].astype(_BF16)], axis=1)
                hu_rem = _dot(xb, w_rem)
                h.append(hu_rem[:, :rem])
                u.append(hu_rem[:, rem:])
            return h, u

        hu = up_proj(0)
        for r in range(0, tm, FFN_RC):
            rows = pl.ds(r, FFN_RC)
            h, u = hu
            if r + FFN_RC < tm:
                hu = up_proj(r + FFN_RC)
            acc = o_ref[rows, :]
            for hk, uk, (k0, k1) in zip(h, u, ((lo, mid), (mid, FFN_TF))):
                a = (jax.nn.silu(hk) * uk).astype(_BF16)
                acc = acc + _dot(a, wd_ref[k0:k1, :].astype(_BF16))
            if finish:
                acc = _ln(0.5 * acc, g_ref[...], b_ref[...])
            o_ref[rows, :] = acc

    @pl.when(j < nf - 1)
    def _():
        accumulate(0, finish=False)

    @pl.when(j == nf - 1)
    def _():
        accumulate(nf * FFN_TF - D_FF, finish=True)


def _ff_window_start(j, base=0):
    lane_tiles = jnp.minimum(j * (FFN_TF // LANES), (D_FF - FFN_TF) // LANES)
    return (lane_tiles + base // LANES) * LANES


def _ffn(x, w_gu, w_down, layer, g, b):
    t = x.shape[0]
    tm = min(FFN_TM, t)
    nf = pl.cdiv(D_FF, FFN_TF)
    col_window = (pl.Squeezed(), pl.Element(D_MODEL), pl.Element(FFN_TF))
    row_window = (pl.Squeezed(), pl.Element(FFN_TF), pl.Element(D_MODEL))
    return pl.pallas_call(
        functools.partial(_ffn_kernel, nf=nf, tm=tm),
        grid=(t // tm, nf),
        in_specs=[
            pl.BlockSpec((tm, D_MODEL), lambda i, j: (i, 0)),
            pl.BlockSpec(col_window, lambda i, j: (layer, 0, _ff_window_start(j))),
            pl.BlockSpec(col_window, lambda i, j: (layer, 0, _ff_window_start(j, base=D_FF))),
            pl.BlockSpec(row_window, lambda i, j: (layer, _ff_window_start(j), 0)),
            pl.BlockSpec((1, D_MODEL), lambda i, j: (0, 0)),
            pl.BlockSpec((1, D_MODEL), lambda i, j: (0, 0)),
        ],
        out_specs=pl.BlockSpec((tm, D_MODEL), lambda i, j: (i, 0)),
        out_shape=jax.ShapeDtypeStruct((t, D_MODEL), _F32),
        compiler_params=pltpu.CompilerParams(
            dimension_semantics=("arbitrary", "arbitrary"),
            vmem_limit_bytes=VMEM_LIMIT_BYTES),
        name="ffn",
    )(x, w_gu, w_gu, w_down, g, b)


def _mixer_a_kernel(x_ref, win_ref, lng_ref, lnb_ref, wm_ref, bias_ref, wout_ref,
                    g_ref, b_ref, *out_and_scratch, sample, tm):
    if sample:
        o_ref, v_ref, mix_ref = out_and_scratch
    else:
        o_ref, mix_ref = out_and_scratch
    sqrt_half = math.sqrt(0.5)

    def gelu(z):
        return 0.5 * z * (1.0 + lax.erf(z * sqrt_half))

    def in_proj(c):
        xb = x_ref[pl.ds(c * CHUNK, CHUNK), :].astype(_BF16)
        return _dot(xb, win_ref[:, :D_MODEL]), _dot(xb, win_ref[:, D_MODEL:])

    row = lax.broadcasted_iota(jnp.int32, (CHUNK, CHUNK), 0)
    col = lax.broadcasted_iota(jnp.int32, (CHUNK, CHUNK), 1)
    mask = col <= row
    if sample:
        mask = jnp.logical_and(mask, (row // SUBLANES) == (col // SUBLANES))

    pre = in_proj(0)
    for c in range(tm // CHUNK):
        rows = pl.ds(c * CHUNK, CHUNK)
        pre_u, pre_v = pre
        if c + 1 < tm // CHUNK:
            pre = in_proj(c + 1)
        u = gelu(pre_u)
        v = _ln(gelu(pre_v), lng_ref[...], lnb_ref[...])
        if sample:
            v_ref[rows, :] = v
        vb = v.astype(_BF16)
        for hd in range(N_GROUPS_A):
            wm = jnp.where(mask, wm_ref[hd], 0.0).astype(_BF16)
            cols = pl.ds(hd * HEAD_A, HEAD_A)
            mix_ref[rows, cols] = _dot(wm, vb[:, hd * HEAD_A:(hd + 1) * HEAD_A])
        gated = (u * (mix_ref[rows, :] + bias_ref[...])).astype(_BF16)
        m = _dot(gated, wout_ref[...])
        o_ref[rows, :] = _ln(ALPHA * x_ref[rows, :] + m, g_ref[...], b_ref[...])


def _mixer_a(x, win, lng, lnb, wm, bias, wout, g, b, *, sample):
    t = x.shape[0]
    tm = MIX_A_TM
    row_spec = pl.BlockSpec((tm, D_MODEL), lambda i: (i, 0))
    out_shape = jax.ShapeDtypeStruct((t, D_MODEL), _F32)
    return pl.pallas_call(
        functools.partial(_mixer_a_kernel, sample=sample, tm=tm),
        grid=(t // tm,),
        in_specs=[
            row_spec,
            _resident((D_MODEL, 2 * D_MODEL)),
            _resident((1, D_MODEL)),
            _resident((1, D_MODEL)),
            _resident((N_GROUPS_A, CHUNK, CHUNK)),
            _resident((CHUNK, D_MODEL)),
            _resident((D_MODEL, D_MODEL)),
            _resident((1, D_MODEL)),
            _resident((1, D_MODEL)),
        ],
        out_specs=(row_spec, row_spec) if sample else row_spec,
        out_shape=(out_shape, out_shape) if sample else out_shape,
        scratch_shapes=[pltpu.VMEM((tm, D_MODEL), _F32)],
        compiler_params=pltpu.CompilerParams(
            dimension_semantics=("arbitrary",), vmem_limit_bytes=MIXER_VMEM_LIMIT_BYTES),
        name="mixer_a_sample" if sample else "mixer_a_prompt",
    )(x, win, lng, lnb, wm, bias, wout, g, b)


def _pool_project(x, win_sums, cnts, w_ref, scale_ref, g_ref, b_ref, o_ref):
    parts = []
    for gi in range(len(POOL_WINDOWS)):
        xg = x[:, gi * POOL_GROUP:(gi + 1) * POOL_GROUP]
        pooled = (win_sums[gi] / cnts[gi] - xg).astype(_BF16)
        parts.append(_dot(pooled, w_ref[gi]))
    m = jnp.concatenate(parts, axis=1) * scale_ref[...]
    o_ref[...] = _ln(ALPHA * x + m, g_ref[...], b_ref[...])


def _mixer_b_prompt_kernel(x_ref, halo_ref, w_ref, scale_ref, g_ref, b_ref, o_ref, xe_ref,
                           *, tm, tiles_per_seq):
    halo_rows = 2 * SUBLANES
    i = pl.program_id(0)
    tile_in_seq = i % tiles_per_seq
    x = x_ref[...]
    xe_ref[pl.ds(0, halo_rows), :] = jnp.where(tile_in_seq == 0, 0.0, halo_ref[...])
    xe_ref[pl.ds(halo_rows, tm), :] = x
    pos = tile_in_seq * tm + lax.broadcasted_iota(jnp.int32, (tm, 1), 0)
    assert POOL_WINDOWS == tuple(2 ** (k + 1) for k in range(len(POOL_WINDOWS)))
    assert POOL_WINDOWS[-1] <= halo_rows
    sums, cnts = [], []
    level = xe_ref[...]
    for gi, w in enumerate(POOL_WINDOWS):
        level = level[:, (POOL_GROUP if gi else 0):]
        level = level + pltpu.roll(level, w // 2, 0)
        sums.append(level[halo_rows:, :POOL_GROUP])
        cnts.append(jnp.minimum(pos + 1, w).astype(_F32))
    _pool_project(x, sums, cnts, w_ref, scale_ref, g_ref, b_ref, o_ref)


def _mixer_b_sample_kernel(x_ref, h1_ref, h2_ref, w_ref, scale_ref, g_ref, b_ref, o_ref, *, tm):
    x = x_ref[...]
    t = lax.broadcasted_iota(jnp.int32, (tm, 1), 0) % SUBLANES
    sums, cnts = [], []
    for gi, w in enumerate(POOL_WINDOWS):
        lo, hi = gi * POOL_GROUP, (gi + 1) * POOL_GROUP
        cur, h1, h2 = x[:, lo:hi], h1_ref[:, lo:hi], h2_ref[:, lo:hi]
        s = cur
        for k in range(1, w):
            kk = k % SUBLANES
            new, old = (cur, h1) if k < SUBLANES else (h1, h2)
            if kk == 0:
                s = s + new
            else:
                s = s + jnp.where(t >= kk, pltpu.roll(new, kk, 0),
                                  pltpu.roll(old, tm + kk - SUBLANES, 0))
        sums.append(s)
        cnts.append(jnp.minimum(PAST_LEN + t + 1, w).astype(_F32))
    _pool_project(x, sums, cnts, w_ref, scale_ref, g_ref, b_ref, o_ref)


def _mixer_b_prompt(x, w, scale, g, b, *, seq):
    t = x.shape[0]
    tm = MIX_TM
    halo_rows = 2 * SUBLANES
    ratio = tm // halo_rows
    return pl.pallas_call(
        functools.partial(_mixer_b_prompt_kernel, tm=tm, tiles_per_seq=seq // tm),
        grid=(t // tm,),
        in_specs=[
            pl.BlockSpec((tm, D_MODEL), lambda i: (i, 0)),
            pl.BlockSpec((halo_rows, D_MODEL), lambda i: (jnp.maximum(i * ratio - 1, 0), 0)),
            _resident((len(POOL_WINDOWS), POOL_GROUP, POOL_GROUP)),
            _resident((1, D_MODEL)),
            _resident((1, D_MODEL)),
            _resident((1, D_MODEL)),
        ],
        out_specs=pl.BlockSpec((tm, D_MODEL), lambda i: (i, 0)),
        out_shape=jax.ShapeDtypeStruct((t, D_MODEL), _F32),
        scratch_shapes=[pltpu.VMEM((tm + halo_rows, D_MODEL), _F32)],
        compiler_params=pltpu.CompilerParams(
            dimension_semantics=("arbitrary",), vmem_limit_bytes=MIXER_VMEM_LIMIT_BYTES),
        name="mixer_b_prompt",
    )(x, x, w, scale, g, b)


def _mixer_b_sample(x, h1, h2, w, scale, g, b):
    t = x.shape[0]
    tm = MIX_TM
    row_spec = pl.BlockSpec((tm, D_MODEL), lambda i: (i, 0))
    return pl.pallas_call(
        functools.partial(_mixer_b_sample_kernel, tm=tm),
        grid=(t // tm,),
        in_specs=[
            row_spec, row_spec, row_spec,
            _resident((len(POOL_WINDOWS), POOL_GROUP, POOL_GROUP)),
            _resident((1, D_MODEL)),
            _resident((1, D_MODEL)),
            _resident((1, D_MODEL)),
        ],
        out_specs=row_spec,
        out_shape=jax.ShapeDtypeStruct((t, D_MODEL), _F32),
        compiler_params=pltpu.CompilerParams(
            dimension_semantics=("arbitrary",), vmem_limit_bytes=MIXER_VMEM_LIMIT_BYTES),
        name="mixer_b_sample",
    )(x, h1, h2, w, scale, g, b)


def _conv_project(x, xb, z, z1, z2, win_ref, wconv_ref, wout_ref, g_ref, b_ref, o_ref):
    conv = wconv_ref[2:3, :] * z + wconv_ref[0:1, :] * z2 + wconv_ref[1:2, :] * z1
    bq = _dot(xb, win_ref[:, :D_MODEL])
    m = _dot((bq * conv).astype(_BF16), wout_ref[...])
    o_ref[...] = _ln(ALPHA * x + m, g_ref[...], b_ref[...])


def _gate_product(xb, win_ref):
    c = _dot(xb, win_ref[:, D_MODEL:2 * D_MODEL])
    h = _dot(xb, win_ref[:, 2 * D_MODEL:])
    return c * h


def _mixer_c_prompt_kernel(x_ref, win_ref, wconv_ref, wout_ref, g_ref, b_ref,
                           o_ref, ztail_ref, ze_ref, carry_ref, *, tm, tiles_per_seq):
    i = pl.program_id(0)
    x = x_ref[...]
    xb = x.astype(_BF16)
    z = _gate_product(xb, win_ref)
    first = (i % tiles_per_seq) == 0
    ze_ref[pl.ds(0, SUBLANES), :] = jnp.where(first, 0.0, carry_ref[...])
    ze_ref[pl.ds(SUBLANES, tm), :] = z
    tail = z[tm - SUBLANES:, :]
    carry_ref[...] = tail
    ztail_ref[...] = tail
    z1 = ze_ref[pl.ds(SUBLANES - 1, tm), :]
    z2 = ze_ref[pl.ds(SUBLANES - 2, tm), :]
    _conv_project(x, xb, z, z1, z2, win_ref, wconv_ref, wout_ref, g_ref, b_ref, o_ref)


def _mixer_c_sample_kernel(x_ref, h1_ref, win_ref, wconv_ref, wout_ref, g_ref, b_ref,
                           o_ref, z_ref, *, tm):
    x = x_ref[...]
    xb = x.astype(_BF16)
    z = _gate_product(xb, win_ref)
    z_ref[...] = z
    t = lax.broadcasted_iota(jnp.int32, (tm, 1), 0) % SUBLANES
    h1 = h1_ref[...]
    z1 = jnp.where(t >= 1, pltpu.roll(z, 1, 0), pltpu.roll(h1, tm + 1 - SUBLANES, 0))
    z2 = jnp.where(t >= 2, pltpu.roll(z, 2, 0), pltpu.roll(h1, tm + 2 - SUBLANES, 0))
    _conv_project(x, xb, z, z1, z2, win_ref, wconv_ref, wout_ref, g_ref, b_ref, o_ref)


def _mixer_c_specs():
    return [
        _resident((D_MODEL, 3 * D_MODEL)),
        _resident((CONV_W, D_MODEL)),
        _resident((D_MODEL, D_MODEL)),
        _resident((1, D_MODEL)),
        _resident((1, D_MODEL)),
    ]


def _mixer_c_prompt(x, win, wconv, wout, g, b, *, seq):
    t = x.shape[0]
    tm = MIX_TM
    row_spec = pl.BlockSpec((tm, D_MODEL), lambda i: (i, 0))
    return pl.pallas_call(
        functools.partial(_mixer_c_prompt_kernel, tm=tm, tiles_per_seq=seq // tm),
        grid=(t // tm,),
        in_specs=[row_spec] + _mixer_c_specs(),
        out_specs=(row_spec, pl.BlockSpec((SUBLANES, D_MODEL), lambda i: (i, 0))),
        out_shape=(jax.ShapeDtypeStruct((t, D_MODEL), _F32),
                   jax.ShapeDtypeStruct((t // tm * SUBLANES, D_MODEL), _F32)),
        scratch_shapes=[pltpu.VMEM((tm + SUBLANES, D_MODEL), _F32),
                        pltpu.VMEM((SUBLANES, D_MODEL), _F32)],
        compiler_params=pltpu.CompilerParams(
            dimension_semantics=("arbitrary",), vmem_limit_bytes=MIXER_VMEM_LIMIT_BYTES),
        name="mixer_c_prompt",
    )(x, win, wconv, wout, g, b)


def _mixer_c_sample(x, h1, win, wconv, wout, g, b):
    t = x.shape[0]
    tm = MIX_TM
    row_spec = pl.BlockSpec((tm, D_MODEL), lambda i: (i, 0))
    out_shape = jax.ShapeDtypeStruct((t, D_MODEL), _F32)
    return pl.pallas_call(
        functools.partial(_mixer_c_sample_kernel, tm=tm),
        grid=(t // tm,),
        in_specs=[row_spec, row_spec] + _mixer_c_specs(),
        out_specs=(row_spec, row_spec),
        out_shape=(out_shape, out_shape),
        compiler_params=pltpu.CompilerParams(
            dimension_semantics=("arbitrary",), vmem_limit_bytes=MIXER_VMEM_LIMIT_BYTES),
        name="mixer_c_sample",
    )(x, h1, win, wconv, wout, g, b)


def _row(v):
    return v.reshape(1, -1)


def _front_pad_hist(hist, rows):
    bsz, h, d = hist.shape
    return jnp.pad(hist, ((0, 0), (rows - h, 0), (0, 0))).reshape(bsz * rows, d)


def kernel(x_prompt, x_sample, state_pool, state_conv, ffn1_w_gu, ffn1_w_down, ffn2_w_gu, ffn2_w_down, ln1_g, ln1_b, ln2_g, ln2_b, ln3_g, ln3_b, a_w_in, a_ln_g, a_ln_b, a_w_s, a_b_s, a_w_out, b_w_grp, b_scale, c_w_in, c_w_conv, c_w_out):
    bp, seq, d = x_prompt.shape
    bs, dec_seq, _ = x_sample.shape
    assert dec_seq == SUBLANES
    assert all(rows % tm == 0 for rows in (seq, bs * dec_seq) for tm in (MIX_TM, MIX_A_TM))
    xp = x_prompt.reshape(bp * seq, d)
    xs = x_sample.reshape(bs * dec_seq, d)
    pool_p, pool_s, conv_p, conv_s, chunk_v_s = [], [], [], [], []
    for i in range(DEPTH):
        kind, j = i % 3, i // 3
        xp = _ffn(xp, ffn1_w_gu, ffn1_w_down, i, _row(ln1_g[i]), _row(ln1_b[i]))
        xs = _ffn(xs, ffn1_w_gu, ffn1_w_down, i, _row(ln1_g[i]), _row(ln1_b[i]))
        g2, b2 = _row(ln2_g[i]), _row(ln2_b[i])
        if kind == 0:
            win = a_w_in[j].astype(_BF16)
            wout = a_w_out[j].astype(_BF16)
            lng, lnb = _row(a_ln_g[j]), _row(a_ln_b[j])
            bias = jnp.repeat(a_b_s[j].T, HEAD_A, axis=1)
            reps = CHUNK // dec_seq
            wm_s = jnp.tile(a_w_s[j][:, :dec_seq, :dec_seq], (1, reps, reps))
            bias_s = jnp.tile(bias[:dec_seq], (reps, 1))
            xp = _mixer_a(xp, win, lng, lnb, a_w_s[j], bias, wout, g2, b2, sample=False)
            xs, vs = _mixer_a(xs, win, lng, lnb, wm_s, bias_s, wout, g2, b2, sample=True)
            chunk_v_s.append(vs.reshape(bs, dec_seq, d))
        elif kind == 1:
            w = b_w_grp[j].astype(_BF16)
            scale = _row(b_scale[j])
            hist = state_pool[j]
            pool_p.append(xp.reshape(bp, seq, d)[:, seq - POOL_HIST:, :])
            pool_s.append(jnp.concatenate(
                [hist[:, dec_seq:, :], xs.reshape(bs, dec_seq, d)], axis=1)[:, -POOL_HIST:, :])
            hist16 = _front_pad_hist(hist, 2 * SUBLANES).reshape(bs, 2, SUBLANES, d)
            h2 = hist16[:, 0].reshape(bs * SUBLANES, d)
            h1 = hist16[:, 1].reshape(bs * SUBLANES, d)
            xp = _mixer_b_prompt(xp, w, scale, g2, b2, seq=seq)
            xs = _mixer_b_sample(xs, h1, h2, w, scale, g2, b2)
        else:
            win = c_w_in[j].astype(_BF16)
            wout = c_w_out[j].astype(_BF16)
            h1 = _front_pad_hist(state_conv[j], SUBLANES)
            xp, ztail = _mixer_c_prompt(xp, win, c_w_conv[j], wout, g2, b2, seq=seq)
            xs, zs = _mixer_c_sample(xs, h1, win, c_w_conv[j], wout, g2, b2)
            tiles_per_seq = seq // MIX_TM
            ztail = ztail.reshape(bp, tiles_per_seq, SUBLANES, d)
            conv_p.append(ztail[:, -1, SUBLANES - (CONV_W - 1):, :])
            conv_s.append(zs.reshape(bs, dec_seq, d)[:, dec_seq - (CONV_W - 1):, :])
        xp = _ffn(xp, ffn2_w_gu, ffn2_w_down, i, _row(ln3_g[i]), _row(ln3_b[i]))
        xs = _ffn(xs, ffn2_w_gu, ffn2_w_down, i, _row(ln3_g[i]), _row(ln3_b[i]))
    return (xp.reshape(bp, seq, d), xs.reshape(bs, dec_seq, d),
            jnp.stack(pool_p), jnp.stack(pool_s), jnp.stack(conv_p), jnp.stack(conv_s),
            jnp.stack(chunk_v_s))
```

```python
import functools
import math

import jax
import jax.numpy as jnp
from jax import lax
from jax.experimental import pallas as pl
from jax.experimental.pallas import tpu as pltpu

D_MODEL = 2048
DEPTH = 4
PAST_LEN = 16384
D_FF = 5504
CHUNK = 128
N_GROUPS_A = 16
HEAD_A = 128
POOL_WINDOWS = (2, 4, 8, 16)
POOL_GROUP = 512
POOL_HIST = 15
CONV_W = 3
ALPHA = (2 * DEPTH) ** 0.25
LN_EPS = 1e-5

SUBLANES = 8
LANES = 128
MXU_TILE = 256
VMEM_LIMIT_BYTES = 62 * 1024 * 1024
MIXER_VMEM_LIMIT_BYTES = 56 * 1024 * 1024

FFN_TM = 1024
FFN_TF = 512
FFN_RC = 256
MIX_TM = 256
MIX_A_TM = 256

_BF16 = jnp.bfloat16
_F32 = jnp.float32


def _ln(y, g, b):
    mu = jnp.mean(y, axis=-1, keepdims=True)
    d = y - mu
    var = jnp.mean(d * d, axis=-1, keepdims=True)
    return d * lax.rsqrt(var + LN_EPS) * g + b


def _dot(a, b):
    return jnp.dot(a, b, preferred_element_type=_F32)


def _resident(shape, layer=None):
    zeros = (0,) * len(shape)
    if layer is None:
        return pl.BlockSpec(shape, lambda *_: zeros, pipeline_mode=pl.Buffered(1))
    return pl.BlockSpec((None,) + tuple(shape), lambda *_: (layer,) + zeros,
                        pipeline_mode=pl.Buffered(1))


def _ffn_kernel(x_ref, wg_ref, wu_ref, wd_ref, g_ref, b_ref, o_ref, *, nf, tm):
    j = pl.program_id(1)

    @pl.when(j == 0)
    def _():
        o_ref[...] = (2.0 * ALPHA) * x_ref[...]

    def accumulate(lo, finish):
        mid = lo + (FFN_TF - lo) // MXU_TILE * MXU_TILE
        rem = FFN_TF - mid
        assert rem in (0, MXU_TILE // 2)

        def up_proj(r):
            xb = x_ref[pl.ds(r, FFN_RC), :].astype(_BF16)
            h = [_dot(xb, wg_ref[:, lo:mid].astype(_BF16))]
            u = [_dot(xb, wu_ref[:, lo:mid].astype(_BF16))]
            if rem:
                w_rem = jnp.concatenate([wg_ref[:, mid:].astype(_BF16),
                                         wu_ref[:, mid:].astype(_BF16)], axis=1)
                hu_rem = _dot(xb, w_rem)
                h.append(hu_rem[:, :rem])
                u.append(hu_rem[:, rem:])
            return h, u

        hu = up_proj(0)
        for r in range(0, tm, FFN_RC):
            rows = pl.ds(r, FFN_RC)
            h, u = hu
            if r + FFN_RC < tm:
                hu = up_proj(r + FFN_RC)
            acc = o_ref[rows, :]
            for hk, uk, (k0, k1) in zip(h, u, ((lo, mid), (mid, FFN_TF))):
                a = (jax.nn.silu(hk) * uk).astype(_BF16)
                acc = acc + _dot(a, wd_ref[k0:k1, :].astype(_BF16))
            if finish:
                acc = _ln(0.5 * acc, g_ref[...], b_ref[...])
            o_ref[rows, :] = acc

    @pl.when(j < nf - 1)
    def _():
        accumulate(0, finish=False)

    @pl.when(j == nf - 1)
    def _():
        accumulate(nf * FFN_TF - D_FF, finish=True)


def _ff_window_start(j, base=0):
    lane_tiles = jnp.minimum(j * (FFN_TF // LANES), (D_FF - FFN_TF) // LANES)
    return (lane_tiles + base // LANES) * LANES


def _ffn(x, w_gu, w_down, layer, g, b):
    t = x.shape[0]
    tm = min(FFN_TM, t)
    nf = pl.cdiv(D_FF, FFN_TF)
    col_window = (pl.Squeezed(), pl.Element(D_MODEL), pl.Element(FFN_TF))
    row_window = (pl.Squeezed(), pl.Element(FFN_TF), pl.Element(D_MODEL))
    return pl.pallas_call(
        functools.partial(_ffn_kernel, nf=nf, tm=tm),
        grid=(t // tm, nf),
        in_specs=[
            pl.BlockSpec((tm, D_MODEL), lambda i, j: (i, 0)),
            pl.BlockSpec(col_window, lambda i, j: (layer, 0, _ff_window_start(j))),
            pl.BlockSpec(col_window, lambda i, j: (layer, 0, _ff_window_start(j, base=D_FF))),
            pl.BlockSpec(row_window, lambda i, j: (layer, _ff_window_start(j), 0)),
            pl.BlockSpec((1, D_MODEL), lambda i, j: (0, 0)),
            pl.BlockSpec((1, D_MODEL), lambda i, j: (0, 0)),
        ],
        out_specs=pl.BlockSpec((tm, D_MODEL), lambda i, j: (i, 0)),
        out_shape=jax.ShapeDtypeStruct((t, D_MODEL), _F32),
        compiler_params=pltpu.CompilerParams(
            dimension_semantics=("arbitrary", "arbitrary"),
            vmem_limit_bytes=VMEM_LIMIT_BYTES),
        name="ffn",
    )(x, w_gu, w_gu, w_down, g, b)


def _mixer_a_kernel(x_ref, win_ref, lng_ref, lnb_ref, wm_ref, bias_ref, wout_ref,
                    g_ref, b_ref, *out_and_scratch, sample, tm):
    if sample:
        o_ref, v_ref, mix_ref = out_and_scratch
    else:
        o_ref, mix_ref = out_and_scratch
    sqrt_half = math.sqrt(0.5)

    def gelu(z):
        return 0.5 * z * (1.0 + lax.erf(z * sqrt_half))

    def in_proj(c):
        xb = x_ref[pl.ds(c * CHUNK, CHUNK), :].astype(_BF16)
        return _dot(xb, win_ref[:, :D_MODEL]), _dot(xb, win_ref[:, D_MODEL:])

    row = lax.broadcasted_iota(jnp.int32, (CHUNK, CHUNK), 0)
    col = lax.broadcasted_iota(jnp.int32, (CHUNK, CHUNK), 1)
    mask = col <= row
    if sample:
        mask = jnp.logical_and(mask, (row // SUBLANES) == (col // SUBLANES))

    pre = in_proj(0)
    for c in range(tm // CHUNK):
        rows = pl.ds(c * CHUNK, CHUNK)
        pre_u, pre_v = pre
        if c + 1 < tm // CHUNK:
            pre = in_proj(c + 1)
        u = gelu(pre_u)
        v = _ln(gelu(pre_v), lng_ref[...], lnb_ref[...])
        if sample:
            v_ref[rows, :] = v
        vb = v.astype(_BF16)
        for hd in range(N_GROUPS_A):
            wm = jnp.where(mask, wm_ref[hd], 0.0).astype(_BF16)
            cols = pl.ds(hd * HEAD_A, HEAD_A)
            mix_ref[rows, cols] = _dot(wm, vb[:, hd * HEAD_A:(hd + 1) * HEAD_A])
        gated = (u * (mix_ref[rows, :] + bias_ref[...])).astype(_BF16)
        m = _dot(gated, wout_ref[...])
        o_ref[rows, :] = _ln(ALPHA * x_ref[rows, :] + m, g_ref[...], b_ref[...])


def _mixer_a(x, win, lng, lnb, wm, bias, wout, layer, g, b, *, sample):
    t = x.shape[0]
    tm = MIX_A_TM
    row_spec = pl.BlockSpec((tm, D_MODEL), lambda i: (i, 0))
    out_shape = jax.ShapeDtypeStruct((t, D_MODEL), _F32)
    return pl.pallas_call(
        functools.partial(_mixer_a_kernel, sample=sample, tm=tm),
        grid=(t // tm,),
        in_specs=[
            row_spec,
            _resident((D_MODEL, 2 * D_MODEL), layer),
            _resident((1, D_MODEL)),
            _resident((1, D_MODEL)),
            _resident((N_GROUPS_A, CHUNK, CHUNK)),
            _resident((CHUNK, D_MODEL)),
            _resident((D_MODEL, D_MODEL), layer),
            _resident((1, D_MODEL)),
            _resident((1, D_MODEL)),
        ],
        out_specs=(row_spec, row_spec) if sample else row_spec,
        out_shape=(out_shape, out_shape) if sample else out_shape,
        scratch_shapes=[pltpu.VMEM((tm, D_MODEL), _F32)],
        compiler_params=pltpu.CompilerParams(
            dimension_semantics=("arbitrary",), vmem_limit_bytes=MIXER_VMEM_LIMIT_BYTES),
        name="mixer_a_sample" if sample else "mixer_a_prompt",
    )(x, win, lng, lnb, wm, bias, wout, g, b)


def _pool_project(x, win_sums, cnts, w_ref, scale_ref, g_ref, b_ref, o_ref):
    parts = []
    for gi in range(len(POOL_WINDOWS)):
        xg = x[:, gi * POOL_GROUP:(gi + 1) * POOL_GROUP]
        pooled = (win_sums[gi] / cnts[gi] - xg).astype(_BF16)
        parts.append(_dot(pooled, w_ref[gi]))
    m = jnp.concatenate(parts, axis=1) * scale_ref[...]
    o_ref[...] = _ln(ALPHA * x + m, g_ref[...], b_ref[...])


def _mixer_b_prompt_kernel(x_ref, halo_ref, w_ref, scale_ref, g_ref, b_ref, o_ref, xe_ref,
                           *, tm, tiles_per_seq):
    halo_rows = 2 * SUBLANES
    i = pl.program_id(0)
    tile_in_seq = i % tiles_per_seq
    x = x_ref[...]
    xe_ref[pl.ds(0, halo_rows), :] = jnp.where(tile_in_seq == 0, 0.0, halo_ref[...])
    xe_ref[pl.ds(halo_rows, tm), :] = x
    pos = tile_in_seq * tm + lax.broadcasted_iota(jnp.int32, (tm, 1), 0)
    assert POOL_WINDOWS == tuple(2 ** (k + 1) for k in range(len(POOL_WINDOWS)))
    assert POOL_WINDOWS[-1] <= halo_rows
    sums, cnts = [], []
    level = xe_ref[...]
    for gi, w in enumerate(POOL_WINDOWS):
        level = level[:, (POOL_GROUP if gi else 0):]
        level = level + pltpu.roll(level, w // 2, 0)
        sums.append(level[halo_rows:, :POOL_GROUP])
        cnts.append(jnp.minimum(pos + 1, w).astype(_F32))
    _pool_project(x, sums, cnts, w_ref, scale_ref, g_ref, b_ref, o_ref)


def _mixer_b_sample_kernel(x_ref, h1_ref, h2_ref, w_ref, scale_ref, g_ref, b_ref, o_ref, *, tm):
    x = x_ref[...]
    t = lax.broadcasted_iota(jnp.int32, (tm, 1), 0) % SUBLANES
    sums, cnts = [], []
    for gi, w in enumerate(POOL_WINDOWS):
        lo, hi = gi * POOL_GROUP, (gi + 1) * POOL_GROUP
        cur, h1, h2 = x[:, lo:hi], h1_ref[:, lo:hi], h2_ref[:, lo:hi]
        s = cur
        for k in range(1, w):
            kk = k % SUBLANES
            new, old = (cur, h1) if k < SUBLANES else (h1, h2)
            if kk == 0:
                s = s + new
            else:
                s = s + jnp.where(t >= kk, pltpu.roll(new, kk, 0),
                                  pltpu.roll(old, tm + kk - SUBLANES, 0))
        sums.append(s)
        cnts.append(jnp.minimum(PAST_LEN + t + 1, w).astype(_F32))
    _pool_project(x, sums, cnts, w_ref, scale_ref, g_ref, b_ref, o_ref)


def _mixer_b_prompt(x, w, scale, g, b, *, seq):
    t = x.shape[0]
    tm = MIX_TM
    halo_rows = 2 * SUBLANES
    ratio = tm // halo_rows
    return pl.pallas_call(
        functools.partial(_mixer_b_prompt_kernel, tm=tm, tiles_per_seq=seq // tm),
        grid=(t // tm,),
        in_specs=[
            pl.BlockSpec((tm, D_MODEL), lambda i: (i, 0)),
            pl.BlockSpec((halo_rows, D_MODEL), lambda i: (jnp.maximum(i * ratio - 1, 0), 0)),
            _resident((len(POOL_WINDOWS), POOL_GROUP, POOL_GROUP)),
            _resident((1, D_MODEL)),
            _resident((1, D_MODEL)),
            _resident((1, D_MODEL)),
        ],
        out_specs=pl.BlockSpec((tm, D_MODEL), lambda i: (i, 0)),
        out_shape=jax.ShapeDtypeStruct((t, D_MODEL), _F32),
        scratch_shapes=[pltpu.VMEM((tm + halo_rows, D_MODEL), _F32)],
        compiler_params=pltpu.CompilerParams(
            dimension_semantics=("arbitrary",), vmem_limit_bytes=MIXER_VMEM_LIMIT_BYTES),
        name="mixer_b_prompt",
    )(x, x, w, scale, g, b)


def _mixer_b_sample(x, h1, h2, w, scale, g, b):
    t = x.shape[0]
    tm = MIX_TM
    row_spec = pl.BlockSpec((tm, D_MODEL), lambda i: (i, 0))
    return pl.pallas_call(
        functools.partial(_mixer_b_sample_kernel, tm=tm),
        grid=(t // tm,),
        in_specs=[
            row_spec, row_spec, row_spec,
            _resident((len(POOL_WINDOWS), POOL_GROUP, POOL_GROUP)),
            _resident((1, D_MODEL)),
            _resident((1, D_MODEL)),
            _resident((1, D_MODEL)),
        ],
        out_specs=row_spec,
        out_shape=jax.ShapeDtypeStruct((t, D_MODEL), _F32),
        compiler_params=pltpu.CompilerParams(
            dimension_semantics=("arbitrary",), vmem_limit_bytes=MIXER_VMEM_LIMIT_BYTES),
        name="mixer_b_sample",
    )(x, h1, h2, w, scale, g, b)


def _conv_project(x, xb, z, z1, z2, win_ref, wconv_ref, wout_ref, g_ref, b_ref, o_ref):
    conv = wconv_ref[2:3, :] * z + wconv_ref[0:1, :] * z2 + wconv_ref[1:2, :] * z1
    bq = _dot(xb, win_ref[:, :D_MODEL])
    m = _dot((bq * conv).astype(_BF16), wout_ref[...])
    o_ref[...] = _ln(ALPHA * x + m, g_ref[...], b_ref[...])


def _gate_product(xb, win_ref):
    c = _dot(xb, win_ref[:, D_MODEL:2 * D_MODEL])
    h = _dot(xb, win_ref[:, 2 * D_MODEL:])
    return c * h


def _mixer_c_prompt_kernel(x_ref, win_ref, wconv_ref, wout_ref, g_ref, b_ref,
                           o_ref, ztail_ref, ze_ref, carry_ref, *, tm, tiles_per_seq):
    i = pl.program_id(0)
    x = x_ref[...]
    xb = x.astype(_BF16)
    z = _gate_product(xb, win_ref)
    first = (i % tiles_per_seq) == 0
    ze_ref[pl.ds(0, SUBLANES), :] = jnp.where(first, 0.0, carry_ref[...])
    ze_ref[pl.ds(SUBLANES, tm), :] = z
    tail = z[tm - SUBLANES:, :]
    carry_ref[...] = tail
    ztail_ref[...] = tail
    z1 = ze_ref[pl.ds(SUBLANES - 1, tm), :]
    z2 = ze_ref[pl.ds(SUBLANES - 2, tm), :]
    _conv_project(x, xb, z, z1, z2, win_ref, wconv_ref, wout_ref, g_ref, b_ref, o_ref)


def _mixer_c_sample_kernel(x_ref, h1_ref, win_ref, wconv_ref, wout_ref, g_ref, b_ref,
                           o_ref, z_ref, *, tm):
    x = x_ref[...]
    xb = x.astype(_BF16)
    z = _gate_product(xb, win_ref)
    z_ref[...] = z
    t = lax.broadcasted_iota(jnp.int32, (tm, 1), 0) % SUBLANES
    h1 = h1_ref[...]
    z1 = jnp.where(t >= 1, pltpu.roll(z, 1, 0), pltpu.roll(h1, tm + 1 - SUBLANES, 0))
    z2 = jnp.where(t >= 2, pltpu.roll(z, 2, 0), pltpu.roll(h1, tm + 2 - SUBLANES, 0))
    _conv_project(x, xb, z, z1, z2, win_ref, wconv_ref, wout_ref, g_ref, b_ref, o_ref)


def _mixer_c_specs():
    return [
        _resident((D_MODEL, 3 * D_MODEL)),
        _resident((CONV_W, D_MODEL)),
        _resident((D_MODEL, D_MODEL)),
        _resident((1, D_MODEL)),
        _resident((1, D_MODEL)),
    ]


def _mixer_c_prompt(x, win, wconv, wout, g, b, *, seq):
    t = x.shape[0]
    tm = MIX_TM
    row_spec = pl.BlockSpec((tm, D_MODEL), lambda i: (i, 0))
    return pl.pallas_call(
        functools.partial(_mixer_c_prompt_kernel, tm=tm, tiles_per_seq=seq // tm),
        grid=(t // tm,),
        in_specs=[row_spec] + _mixer_c_specs(),
        out_specs=(row_spec, pl.BlockSpec((SUBLANES, D_MODEL), lambda i: (i, 0))),
        out_shape=(jax.ShapeDtypeStruct((t, D_MODEL), _F32),
                   jax.ShapeDtypeStruct((t // tm * SUBLANES, D_MODEL), _F32)),
        scratch_shapes=[pltpu.VMEM((tm + SUBLANES, D_MODEL), _F32),
                        pltpu.VMEM((SUBLANES, D_MODEL), _F32)],
        compiler_params=pltpu.CompilerParams(
            dimension_semantics=("arbitrary",), vmem_limit_bytes=MIXER_VMEM_LIMIT_BYTES),
        name="mixer_c_prompt",
    )(x, win, wconv, wout, g, b)


def _mixer_c_sample(x, h1, win, wconv, wout, g, b):
    t = x.shape[0]
    tm = MIX_TM
    row_spec = pl.BlockSpec((tm, D_MODEL), lambda i: (i, 0))
    out_shape = jax.ShapeDtypeStruct((t, D_MODEL), _F32)
    return pl.pallas_call(
        functools.partial(_mixer_c_sample_kernel, tm=tm),
        grid=(t // tm,),
        in_specs=[row_spec, row_spec] + _mixer_c_specs(),
        out_specs=(row_spec, row_spec),
        out_shape=(out_shape, out_shape),
        compiler_params=pltpu.CompilerParams(
            dimension_semantics=("arbitrary",), vmem_limit_bytes=MIXER_VMEM_LIMIT_BYTES),
        name="mixer_c_sample",
    )(x, h1, win, wconv, wout, g, b)


def _row(v):
    return v.reshape(1, -1)


def _front_pad_hist(hist, rows):
    bsz, h, d = hist.shape
    return jnp.pad(hist, ((0, 0), (rows - h, 0), (0, 0))).reshape(bsz * rows, d)


def kernel(x_prompt, x_sample, state_pool, state_conv, ffn1_w_gu, ffn1_w_down, ffn2_w_gu, ffn2_w_down, ln1_g, ln1_b, ln2_g, ln2_b, ln3_g, ln3_b, a_w_in, a_ln_g, a_ln_b, a_w_s, a_b_s, a_w_out, b_w_grp, b_scale, c_w_in, c_w_conv, c_w_out):
    bp, seq, d = x_prompt.shape
    bs, dec_seq, _ = x_sample.shape
    assert dec_seq == SUBLANES
    assert all(rows % tm == 0 for rows in (seq, bs * dec_seq) for tm in (MIX_TM, MIX_A_TM))
    xp = x_prompt.reshape(bp * seq, d)
    xs = x_sample.reshape(bs * dec_seq, d)
    pool_p, pool_s, conv_p, conv_s, chunk_v_s = [], [], [], [], []
    a_win_bf16, a_wout_bf16 = a_w_in.astype(_BF16), a_w_out.astype(_BF16)
    for i in range(DEPTH):
        kind, j = i % 3, i // 3
        xp = _ffn(xp, ffn1_w_gu, ffn1_w_down, i, _row(ln1_g[i]), _row(ln1_b[i]))
        xs = _ffn(xs, ffn1_w_gu, ffn1_w_down, i, _row(ln1_g[i]), _row(ln1_b[i]))
        g2, b2 = _row(ln2_g[i]), _row(ln2_b[i])
        if kind == 0:
            win, wout = a_win_bf16, a_wout_bf16
            lng, lnb = _row(a_ln_g[j]), _row(a_ln_b[j])
            bias = jnp.repeat(a_b_s[j].T, HEAD_A, axis=1)
            reps = CHUNK // dec_seq
            wm_s = jnp.tile(a_w_s[j][:, :dec_seq, :dec_seq], (1, reps, reps))
            bias_s = jnp.tile(bias[:dec_seq], (reps, 1))
            xp = _mixer_a(xp, win, lng, lnb, a_w_s[j], bias, wout, j, g2, b2, sample=False)
            xs, vs = _mixer_a(xs, win, lng, lnb, wm_s, bias_s, wout, j, g2, b2, sample=True)
            chunk_v_s.append(vs.reshape(bs, dec_seq, d))
        elif kind == 1:
            w = b_w_grp[j].astype(_BF16)
            scale = _row(b_scale[j])
            hist = state_pool[j]
            pool_p.append(xp.reshape(bp, seq, d)[:, seq - POOL_HIST:, :])
            pool_s.append(jnp.concatenate(
                [hist[:, dec_seq:, :], xs.reshape(bs, dec_seq, d)], axis=1)[:, -POOL_HIST:, :])
            h1 = hist[:, POOL_HIST - SUBLANES:, :].reshape(bs * SUBLANES, d)
            h2 = _front_pad_hist(hist[:, :POOL_HIST - SUBLANES, :], SUBLANES)
            xp = _mixer_b_prompt(xp, w, scale, g2, b2, seq=seq)
            xs = _mixer_b_sample(xs, h1, h2, w, scale, g2, b2)
        else:
            win = c_w_in[j].astype(_BF16)
            wout = c_w_out[j].astype(_BF16)
            h1 = _front_pad_hist(state_conv[j], SUBLANES)
            xp, ztail = _mixer_c_prompt(xp, win, c_w_conv[j], wout, g2, b2, seq=seq)
            xs, zs = _mixer_c_sample(xs, h1, win, c_w_conv[j], wout, g2, b2)
            tiles_per_seq = seq // MIX_TM
            ztail = ztail.reshape(bp, tiles_per_seq, SUBLANES, d)
            conv_p.append(ztail[:, -1, SUBLANES - (CONV_W - 1):, :])
            conv_s.append(zs.reshape(bs, dec_seq, d)[:, dec_seq - (CONV_W - 1):, :])
        xp = _ffn(xp, ffn2_w_gu, ffn2_w_down, i, _row(ln3_g[i]), _row(ln3_b[i]))
        xs = _ffn(xs, ffn2_w_gu, ffn2_w_down, i, _row(ln3_g[i]), _row(ln3_b[i]))
    return (xp.reshape(bp, seq, d), xs.reshape(bs, dec_seq, d),
            jnp.stack(pool_p), jnp.stack(pool_s), jnp.stack(conv_p), jnp.stack(conv_s),
            jnp.stack(chunk_v_s))
```

```python
import functools
import math

import jax
import jax.numpy as jnp
from jax import lax
from jax.experimental import pallas as pl
from jax.experimental.pallas import tpu as pltpu

D_MODEL = 2048
DEPTH = 4
PAST_LEN = 16384
D_FF = 5504
CHUNK = 128
N_GROUPS_A = 16
HEAD_A = 128
POOL_WINDOWS = (2, 4, 8, 16)
POOL_GROUP = 512
POOL_HIST = 15
CONV_W = 3
ALPHA = (2 * DEPTH) ** 0.25
LN_EPS = 1e-5

SUBLANES = 8
LANES = 128
MXU_TILE = 256
VMEM_LIMIT_BYTES = 62 * 1024 * 1024
MIXER_VMEM_LIMIT_BYTES = 56 * 1024 * 1024

FFN_TM = 1024
FFN_TF = 512
FFN_RC = 256
MIX_TM = 256
MIX_A_TM = 256

_BF16 = jnp.bfloat16
_F32 = jnp.float32


def _ln(y, g, b):
    mu = jnp.mean(y, axis=-1, keepdims=True)
    d = y - mu
    var = jnp.mean(d * d, axis=-1, keepdims=True)
    return d * lax.rsqrt(var + LN_EPS) * g + b


def _dot(a, b):
    return jnp.dot(a, b, preferred_element_type=_F32)


def _resident(shape, layer=None):
    zeros = (0,) * len(shape)
    if layer is None:
        return pl.BlockSpec(shape, lambda *_: zeros, pipeline_mode=pl.Buffered(1))
    return pl.BlockSpec((None,) + tuple(shape), lambda *_: (layer,) + zeros,
                        pipeline_mode=pl.Buffered(1))


def _ffn_kernel(x_ref, wg_ref, wu_ref, wd_ref, g_ref, b_ref, o_ref, *, nf, tm):
    j = pl.program_id(1)

    def accumulate(lo, first=False, finish=False):
        mid = lo + (FFN_TF - lo) // MXU_TILE * MXU_TILE
        rem = FFN_TF - mid
        assert rem in (0, MXU_TILE // 2)

        def up_proj(r):
            xb = x_ref[pl.ds(r, FFN_RC), :].astype(_BF16)
            h = [_dot(xb, wg_ref[:, lo:mid].astype(_BF16))]
            u = [_dot(xb, wu_ref[:, lo:mid].astype(_BF16))]
            if rem:
                w_rem = jnp.concatenate([wg_ref[:, mid:].astype(_BF16),
                                         wu_ref[:, mid:].astype(_BF16)], axis=1)
                hu_rem = _dot(xb, w_rem)
                h.append(hu_rem[:, :rem])
                u.append(hu_rem[:, rem:])
            return h, u

        hu = up_proj(0)
        for r in range(0, tm, FFN_RC):
            rows = pl.ds(r, FFN_RC)
            h, u = hu
            if r + FFN_RC < tm:
                hu = up_proj(r + FFN_RC)
            acc = (2.0 * ALPHA) * x_ref[rows, :] if first else o_ref[rows, :]
            for hk, uk, (k0, k1) in zip(h, u, ((lo, mid), (mid, FFN_TF))):
                a = (jax.nn.silu(hk) * uk).astype(_BF16)
                acc = acc + _dot(a, wd_ref[k0:k1, :].astype(_BF16))
            if finish:
                acc = _ln(0.5 * acc, g_ref[...], b_ref[...])
            o_ref[rows, :] = acc

    assert nf > 2

    @pl.when(j == 0)
    def _():
        accumulate(0, first=True)

    @pl.when(jnp.logical_and(j > 0, j < nf - 1))
    def _():
        accumulate(0)

    @pl.when(j == nf - 1)
    def _():
        accumulate(nf * FFN_TF - D_FF, finish=True)


def _ff_window_start(j, base=0):
    lane_tiles = jnp.minimum(j * (FFN_TF // LANES), (D_FF - FFN_TF) // LANES)
    return (lane_tiles + base // LANES) * LANES


def _ffn(x, w_gu, w_down, layer, g, b):
    t = x.shape[0]
    tm = min(FFN_TM, t)
    nf = pl.cdiv(D_FF, FFN_TF)
    col_window = (pl.Squeezed(), pl.Element(D_MODEL), pl.Element(FFN_TF))
    row_window = (pl.Squeezed(), pl.Element(FFN_TF), pl.Element(D_MODEL))
    return pl.pallas_call(
        functools.partial(_ffn_kernel, nf=nf, tm=tm),
        grid=(t // tm, nf),
        in_specs=[
            pl.BlockSpec((tm, D_MODEL), lambda i, j: (i, 0)),
            pl.BlockSpec(col_window, lambda i, j: (layer, 0, _ff_window_start(j))),
            pl.BlockSpec(col_window, lambda i, j: (layer, 0, _ff_window_start(j, base=D_FF))),
            pl.BlockSpec(row_window, lambda i, j: (layer, _ff_window_start(j), 0)),
            pl.BlockSpec((1, D_MODEL), lambda i, j: (0, 0)),
            pl.BlockSpec((1, D_MODEL), lambda i, j: (0, 0)),
        ],
        out_specs=pl.BlockSpec((tm, D_MODEL), lambda i, j: (i, 0)),
        out_shape=jax.ShapeDtypeStruct((t, D_MODEL), _F32),
        compiler_params=pltpu.CompilerParams(
            dimension_semantics=("arbitrary", "arbitrary"),
            vmem_limit_bytes=VMEM_LIMIT_BYTES),
        name="ffn",
    )(x, w_gu, w_gu, w_down, g, b)


def _mixer_a_kernel(x_ref, win_ref, lng_ref, lnb_ref, wm_ref, bias_ref, wout_ref,
                    g_ref, b_ref, *out_and_scratch, sample, tm):
    if sample:
        o_ref, v_ref, mix_ref = out_and_scratch
    else:
        o_ref, mix_ref = out_and_scratch
    sqrt_half = math.sqrt(0.5)

    def gelu(z):
        return 0.5 * z * (1.0 + lax.erf(z * sqrt_half))

    def in_proj(c):
        xb = x_ref[pl.ds(c * CHUNK, CHUNK), :].astype(_BF16)
        return _dot(xb, win_ref[:, :D_MODEL]), _dot(xb, win_ref[:, D_MODEL:])

    row = lax.broadcasted_iota(jnp.int32, (CHUNK, CHUNK), 0)
    col = lax.broadcasted_iota(jnp.int32, (CHUNK, CHUNK), 1)
    mask = col <= row
    if sample:
        mask = jnp.logical_and(mask, (row // SUBLANES) == (col // SUBLANES))

    pre = in_proj(0)
    for c in range(tm // CHUNK):
        rows = pl.ds(c * CHUNK, CHUNK)
        pre_u, pre_v = pre
        if c + 1 < tm // CHUNK:
            pre = in_proj(c + 1)
        u = gelu(pre_u)
        v = _ln(gelu(pre_v), lng_ref[...], lnb_ref[...])
        if sample:
            v_ref[rows, :] = v
        vb = v.astype(_BF16)
        for hd in range(N_GROUPS_A):
            wm = jnp.where(mask, wm_ref[hd], 0.0).astype(_BF16)
            cols = pl.ds(hd * HEAD_A, HEAD_A)
            mix_ref[rows, cols] = _dot(wm, vb[:, hd * HEAD_A:(hd + 1) * HEAD_A])
        gated = (u * (mix_ref[rows, :] + bias_ref[...])).astype(_BF16)
        m = _dot(gated, wout_ref[...])
        o_ref[rows, :] = _ln(ALPHA * x_ref[rows, :] + m, g_ref[...], b_ref[...])


def _mixer_a(x, win, lng, lnb, wm, bias, wout, layer, g, b, *, sample):
    t = x.shape[0]
    tm = MIX_A_TM
    row_spec = pl.BlockSpec((tm, D_MODEL), lambda i: (i, 0))
    out_shape = jax.ShapeDtypeStruct((t, D_MODEL), _F32)
    return pl.pallas_call(
        functools.partial(_mixer_a_kernel, sample=sample, tm=tm),
        grid=(t // tm,),
        in_specs=[
            row_spec,
            _resident((D_MODEL, 2 * D_MODEL), layer),
            _resident((1, D_MODEL)),
            _resident((1, D_MODEL)),
            _resident((N_GROUPS_A, CHUNK, CHUNK)),
            _resident((CHUNK, D_MODEL)),
            _resident((D_MODEL, D_MODEL), layer),
            _resident((1, D_MODEL)),
            _resident((1, D_MODEL)),
        ],
        out_specs=(row_spec, row_spec) if sample else row_spec,
        out_shape=(out_shape, out_shape) if sample else out_shape,
        scratch_shapes=[pltpu.VMEM((tm, D_MODEL), _F32)],
        compiler_params=pltpu.CompilerParams(
            dimension_semantics=("arbitrary",), vmem_limit_bytes=MIXER_VMEM_LIMIT_BYTES),
        name="mixer_a_sample" if sample else "mixer_a_prompt",
    )(x, win, lng, lnb, wm, bias, wout, g, b)


def _pool_project(x, win_sums, cnts, w_ref, scale_ref, g_ref, b_ref, o_ref):
    parts = []
    for gi in range(len(POOL_WINDOWS)):
        xg = x[:, gi * POOL_GROUP:(gi + 1) * POOL_GROUP]
        pooled = (win_sums[gi] / cnts[gi] - xg).astype(_BF16)
        parts.append(_dot(pooled, w_ref[gi]))
    m = jnp.concatenate(parts, axis=1) * scale_ref[...]
    o_ref[...] = _ln(ALPHA * x + m, g_ref[...], b_ref[...])


def _mixer_b_prompt_kernel(x_ref, halo_ref, w_ref, scale_ref, g_ref, b_ref, o_ref, xe_ref,
                           *, tm, tiles_per_seq):
    halo_rows = 2 * SUBLANES
    i = pl.program_id(0)
    tile_in_seq = i % tiles_per_seq
    x = x_ref[...]
    xe_ref[pl.ds(0, halo_rows), :] = jnp.where(tile_in_seq == 0, 0.0, halo_ref[...])
    xe_ref[pl.ds(halo_rows, tm), :] = x
    pos = tile_in_seq * tm + lax.broadcasted_iota(jnp.int32, (tm, 1), 0)
    assert POOL_WINDOWS == tuple(2 ** (k + 1) for k in range(len(POOL_WINDOWS)))
    assert POOL_WINDOWS[-1] <= halo_rows
    sums, cnts = [], []
    level = xe_ref[...]
    for gi, w in enumerate(POOL_WINDOWS):
        level = level[:, (POOL_GROUP if gi else 0):]
        level = level + pltpu.roll(level, w // 2, 0)
        sums.append(level[halo_rows:, :POOL_GROUP])
        cnts.append(jnp.minimum(pos + 1, w).astype(_F32))
    _pool_project(x, sums, cnts, w_ref, scale_ref, g_ref, b_ref, o_ref)


def _mixer_b_sample_kernel(x_ref, h1_ref, h2_ref, w_ref, scale_ref, g_ref, b_ref, o_ref, *, tm):
    x = x_ref[...]
    t = lax.broadcasted_iota(jnp.int32, (tm, 1), 0) % SUBLANES
    sums, cnts = [], []
    for gi, w in enumerate(POOL_WINDOWS):
        lo, hi = gi * POOL_GROUP, (gi + 1) * POOL_GROUP
        cur, h1, h2 = x[:, lo:hi], h1_ref[:, lo:hi], h2_ref[:, lo:hi]
        s = cur
        for k in range(1, w):
            kk = k % SUBLANES
            new, old = (cur, h1) if k < SUBLANES else (h1, h2)
            if kk == 0:
                s = s + new
            else:
                s = s + jnp.where(t >= kk, pltpu.roll(new, kk, 0),
                                  pltpu.roll(old, tm + kk - SUBLANES, 0))
        sums.append(s)
        cnts.append(jnp.minimum(PAST_LEN + t + 1, w).astype(_F32))
    _pool_project(x, sums, cnts, w_ref, scale_ref, g_ref, b_ref, o_ref)


def _mixer_b_prompt(x, w, scale, g, b, *, seq):
    t = x.shape[0]
    tm = MIX_TM
    halo_rows = 2 * SUBLANES
    ratio = tm // halo_rows
    return pl.pallas_call(
        functools.partial(_mixer_b_prompt_kernel, tm=tm, tiles_per_seq=seq // tm),
        grid=(t // tm,),
        in_specs=[
            pl.BlockSpec((tm, D_MODEL), lambda i: (i, 0)),
            pl.BlockSpec((halo_rows, D_MODEL), lambda i: (jnp.maximum(i * ratio - 1, 0), 0)),
            _resident((len(POOL_WINDOWS), POOL_GROUP, POOL_GROUP)),
            _resident((1, D_MODEL)),
            _resident((1, D_MODEL)),
            _resident((1, D_MODEL)),
        ],
        out_specs=pl.BlockSpec((tm, D_MODEL), lambda i: (i, 0)),
        out_shape=jax.ShapeDtypeStruct((t, D_MODEL), _F32),
        scratch_shapes=[pltpu.VMEM((tm + halo_rows, D_MODEL), _F32)],
        compiler_params=pltpu.CompilerParams(
            dimension_semantics=("arbitrary",), vmem_limit_bytes=MIXER_VMEM_LIMIT_BYTES),
        name="mixer_b_prompt",
    )(x, x, w, scale, g, b)


def _mixer_b_sample(x, h1, h2, w, scale, g, b):
    t = x.shape[0]
    tm = MIX_TM
    row_spec = pl.BlockSpec((tm, D_MODEL), lambda i: (i, 0))
    return pl.pallas_call(
        functools.partial(_mixer_b_sample_kernel, tm=tm),
        grid=(t // tm,),
        in_specs=[
            row_spec, row_spec, row_spec,
            _resident((len(POOL_WINDOWS), POOL_GROUP, POOL_GROUP)),
            _resident((1, D_MODEL)),
            _resident((1, D_MODEL)),
            _resident((1, D_MODEL)),
        ],
        out_specs=row_spec,
        out_shape=jax.ShapeDtypeStruct((t, D_MODEL), _F32),
        compiler_params=pltpu.CompilerParams(
            dimension_semantics=("arbitrary",), vmem_limit_bytes=MIXER_VMEM_LIMIT_BYTES),
        name="mixer_b_sample",
    )(x, h1, h2, w, scale, g, b)


def _conv_project(x, xb, z, z1, z2, win_ref, wconv_ref, wout_ref, g_ref, b_ref, o_ref):
    conv = wconv_ref[2:3, :] * z + wconv_ref[0:1, :] * z2 + wconv_ref[1:2, :] * z1
    bq = _dot(xb, win_ref[:, :D_MODEL])
    m = _dot((bq * conv).astype(_BF16), wout_ref[...])
    o_ref[...] = _ln(ALPHA * x + m, g_ref[...], b_ref[...])


def _gate_product(xb, win_ref):
    c = _dot(xb, win_ref[:, D_MODEL:2 * D_MODEL])
    h = _dot(xb, win_ref[:, 2 * D_MODEL:])
    return c * h


def _mixer_c_prompt_kernel(x_ref, win_ref, wconv_ref, wout_ref, g_ref, b_ref,
                           o_ref, ztail_ref, ze_ref, carry_ref, *, tm, tiles_per_seq):
    i = pl.program_id(0)
    x = x_ref[...]
    xb = x.astype(_BF16)
    z = _gate_product(xb, win_ref)
    first = (i % tiles_per_seq) == 0
    ze_ref[pl.ds(0, SUBLANES), :] = jnp.where(first, 0.0, carry_ref[...])
    ze_ref[pl.ds(SUBLANES, tm), :] = z
    tail = z[tm - SUBLANES:, :]
    carry_ref[...] = tail
    ztail_ref[...] = tail
    z1 = ze_ref[pl.ds(SUBLANES - 1, tm), :]
    z2 = ze_ref[pl.ds(SUBLANES - 2, tm), :]
    _conv_project(x, xb, z, z1, z2, win_ref, wconv_ref, wout_ref, g_ref, b_ref, o_ref)


def _mixer_c_sample_kernel(x_ref, h1_ref, win_ref, wconv_ref, wout_ref, g_ref, b_ref,
                           o_ref, z_ref, *, tm):
    x = x_ref[...]
    xb = x.astype(_BF16)
    z = _gate_product(xb, win_ref)
    z_ref[...] = z
    t = lax.broadcasted_iota(jnp.int32, (tm, 1), 0) % SUBLANES
    h1 = h1_ref[...]
    z1 = jnp.where(t >= 1, pltpu.roll(z, 1, 0), pltpu.roll(h1, tm + 1 - SUBLANES, 0))
    z2 = jnp.where(t >= 2, pltpu.roll(z, 2, 0), pltpu.roll(h1, tm + 2 - SUBLANES, 0))
    _conv_project(x, xb, z, z1, z2, win_ref, wconv_ref, wout_ref, g_ref, b_ref, o_ref)


def _mixer_c_specs():
    return [
        _resident((D_MODEL, 3 * D_MODEL)),
        _resident((CONV_W, D_MODEL)),
        _resident((D_MODEL, D_MODEL)),
        _resident((1, D_MODEL)),
        _resident((1, D_MODEL)),
    ]


def _mixer_c_prompt(x, win, wconv, wout, g, b, *, seq):
    t = x.shape[0]
    tm = MIX_TM
    row_spec = pl.BlockSpec((tm, D_MODEL), lambda i: (i, 0))
    return pl.pallas_call(
        functools.partial(_mixer_c_prompt_kernel, tm=tm, tiles_per_seq=seq // tm),
        grid=(t // tm,),
        in_specs=[row_spec] + _mixer_c_specs(),
        out_specs=(row_spec, pl.BlockSpec((SUBLANES, D_MODEL), lambda i: (i, 0))),
        out_shape=(jax.ShapeDtypeStruct((t, D_MODEL), _F32),
                   jax.ShapeDtypeStruct((t // tm * SUBLANES, D_MODEL), _F32)),
        scratch_shapes=[pltpu.VMEM((tm + SUBLANES, D_MODEL), _F32),
                        pltpu.VMEM((SUBLANES, D_MODEL), _F32)],
        compiler_params=pltpu.CompilerParams(
            dimension_semantics=("arbitrary",), vmem_limit_bytes=MIXER_VMEM_LIMIT_BYTES),
        name="mixer_c_prompt",
    )(x, win, wconv, wout, g, b)


def _mixer_c_sample(x, h1, win, wconv, wout, g, b):
    t = x.shape[0]
    tm = MIX_TM
    row_spec = pl.BlockSpec((tm, D_MODEL), lambda i: (i, 0))
    out_shape = jax.ShapeDtypeStruct((t, D_MODEL), _F32)
    return pl.pallas_call(
        functools.partial(_mixer_c_sample_kernel, tm=tm),
        grid=(t // tm,),
        in_specs=[row_spec, row_spec] + _mixer_c_specs(),
        out_specs=(row_spec, row_spec),
        out_shape=(out_shape, out_shape),
        compiler_params=pltpu.CompilerParams(
            dimension_semantics=("arbitrary",), vmem_limit_bytes=MIXER_VMEM_LIMIT_BYTES),
        name="mixer_c_sample",
    )(x, h1, win, wconv, wout, g, b)


def _row(v):
    return v.reshape(1, -1)


def _front_pad_hist(hist, rows):
    bsz, h, d = hist.shape
    return jnp.pad(hist, ((0, 0), (rows - h, 0), (0, 0))).reshape(bsz * rows, d)


def kernel(x_prompt, x_sample, state_pool, state_conv, ffn1_w_gu, ffn1_w_down, ffn2_w_gu, ffn2_w_down, ln1_g, ln1_b, ln2_g, ln2_b, ln3_g, ln3_b, a_w_in, a_ln_g, a_ln_b, a_w_s, a_b_s, a_w_out, b_w_grp, b_scale, c_w_in, c_w_conv, c_w_out):
    bp, seq, d = x_prompt.shape
    bs, dec_seq, _ = x_sample.shape
    assert dec_seq == SUBLANES
    assert all(rows % tm == 0 for rows in (seq, bs * dec_seq) for tm in (MIX_TM, MIX_A_TM))
    xp = x_prompt.reshape(bp * seq, d)
    xs = x_sample.reshape(bs * dec_seq, d)
    pool_p, pool_s, conv_p, conv_s, chunk_v_s = [], [], [], [], []
    a_win_bf16, a_wout_bf16 = a_w_in.astype(_BF16), a_w_out.astype(_BF16)
    for i in range(DEPTH):
        kind, j = i % 3, i // 3
        xp = _ffn(xp, ffn1_w_gu, ffn1_w_down, i, _row(ln1_g[i]), _row(ln1_b[i]))
        xs = _ffn(xs, ffn1_w_gu, ffn1_w_down, i, _row(ln1_g[i]), _row(ln1_b[i]))
        g2, b2 = _row(ln2_g[i]), _row(ln2_b[i])
        if kind == 0:
            win, wout = a_win_bf16, a_wout_bf16
            lng, lnb = _row(a_ln_g[j]), _row(a_ln_b[j])
            bias = jnp.repeat(a_b_s[j].T, HEAD_A, axis=1)
            reps = CHUNK // dec_seq
            wm_s = jnp.tile(a_w_s[j][:, :dec_seq, :dec_seq], (1, reps, reps))
            bias_s = jnp.tile(bias[:dec_seq], (reps, 1))
            xp = _mixer_a(xp, win, lng, lnb, a_w_s[j], bias, wout, j, g2, b2, sample=False)
            xs, vs = _mixer_a(xs, win, lng, lnb, wm_s, bias_s, wout, j, g2, b2, sample=True)
            chunk_v_s.append(vs.reshape(bs, dec_seq, d))
        elif kind == 1:
            w = b_w_grp[j].astype(_BF16)
            scale = _row(b_scale[j])
            hist = state_pool[j]
            pool_p.append(xp.reshape(bp, seq, d)[:, seq - POOL_HIST:, :])
            pool_s.append(jnp.concatenate(
                [hist[:, dec_seq:, :], xs.reshape(bs, dec_seq, d)], axis=1)[:, -POOL_HIST:, :])
            h1 = hist[:, POOL_HIST - SUBLANES:, :].reshape(bs * SUBLANES, d)
            h2 = _front_pad_hist(hist[:, :POOL_HIST - SUBLANES, :], SUBLANES)
            xp = _mixer_b_prompt(xp, w, scale, g2, b2, seq=seq)
            xs = _mixer_b_sample(xs, h1, h2, w, scale, g2, b2)
        else:
            win = c_w_in[j].astype(_BF16)
            wout = c_w_out[j].astype(_BF16)
            h1 = _front_pad_hist(state_conv[j], SUBLANES)
            xp, ztail = _mixer_c_prompt(xp, win, c_w_conv[j], wout, g2, b2, seq=seq)
            xs, zs = _mixer_c_sample(xs, h1, win, c_w_conv[j], wout, g2, b2)
            tiles_per_seq = seq // MIX_TM
            ztail = ztail.reshape(bp, tiles_per_seq, SUBLANES, d)
            conv_p.append(ztail[:, -1, SUBLANES - (CONV_W - 1):, :])
            conv_s.append(zs.reshape(bs, dec_seq, d)[:, dec_seq - (CONV_W - 1):, :])
        xp = _ffn(xp, ffn2_w_gu, ffn2_w_down, i, _row(ln3_g[i]), _row(ln3_b[i]))
        xs = _ffn(xs, ffn2_w_gu, ffn2_w_down, i, _row(ln3_g[i]), _row(ln3_b[i]))
    return (xp.reshape(bp, seq, d), xs.reshape(bs, dec_seq, d),
            jnp.stack(pool_p), jnp.stack(pool_s), jnp.stack(conv_p), jnp.stack(conv_s),
            jnp.stack(chunk_v_s))
```
